```python
import math
import jax
import jax.numpy as jnp
from jax import lax
import numpy as np

D_MODEL = 4096
BATCH = 2
SEQ = 4096
DEPTH = 1
DEC_BATCH = 128
DEC_SEQ = 4
PAST_LEN = 2048
PAGE_SIZE = 128

D_SSM = D_MODEL // 2
SSM_CH = 16
SSM_GROUPS = D_SSM // SSM_CH
SSM_STATE = 64
DT_MIN = 1e-3
DT_MAX = 1e-1
N_HEADS = 16
HEAD_DIM = (D_MODEL - D_SSM) // N_HEADS
N_KV_HEADS = 4
GQ = N_HEADS // N_KV_HEADS
N_KV_SLOTS = 4
BLOCK = 64
N_SEL = 16
WINDOW = 512
CMP_HIDDEN = 256
Q_BLOCK = 128
FORCE_BONUS = 1e4
N_EXPERT_GROUPS = 4
EXPERTS_PER_GROUP = 8
N_EXPERTS = N_EXPERT_GROUPS * EXPERTS_PER_GROUP
TOP_K_INNER = 2
D_EXPERT = 512
D_IN = D_SSM + N_HEADS * HEAD_DIM + 6 * N_KV_HEADS * HEAD_DIM + 3 * N_HEADS
EPS = 1e-6
NEG = -1e30

kernel_name = 'hybrid_s5_nsa_hmoe_step'


def rms_norm(x, g):
    xf = x.astype(jnp.float32)
    y = xf * lax.rsqrt(jnp.mean(xf * xf, axis=-1, keepdims=True) + EPS)
    return (y * g.astype(jnp.float32)).astype(x.dtype)


def modulate(x, g, shift, scale):
    return rms_norm(x, g) * (1.0 + scale) + shift


def ada_terms(c, w_ada, b_ada):
    m = jax.nn.silu(c) @ w_ada + b_ada
    return jnp.split(m[:, None, :], 6, axis=-1)


def s5_discretize(a_re, a_im, log_dt, b_re, b_im):
    lam_re = jnp.minimum(a_re.astype(jnp.float32), -1e-4)
    lam_im = a_im.astype(jnp.float32)
    dt = jnp.exp(log_dt.astype(jnp.float32))[:, None]
    mag = jnp.exp(lam_re * dt)
    ab_re = mag * jnp.cos(lam_im * dt)
    ab_im = mag * jnp.sin(lam_im * dt)
    den = lam_re * lam_re + lam_im * lam_im
    n_re = ab_re - 1.0
    f_re = (n_re * lam_re + ab_im * lam_im) / den
    f_im = (ab_im * lam_re - n_re * lam_im) / den
    br = b_re.astype(jnp.float32)
    bi = b_im.astype(jnp.float32)
    bb_re = f_re[..., None] * br - f_im[..., None] * bi
    bb_im = f_re[..., None] * bi + f_im[..., None] * br
    return ab_re, ab_im, bb_re, bb_im


def _cscan_combine(e1, e2):
    a1r, a1i, b1r, b1i = e1
    a2r, a2i, b2r, b2i = e2
    return (a2r * a1r - a2i * a1i, a2r * a1i + a2i * a1r,
            a2r * b1r - a2i * b1i + b2r, a2r * b1i + a2i * b1r + b2i)


def s5_mixer(u, h0_re, h0_im, disc, c_re, c_im, d_skip, w_glu, b_glu):
    ab_re, ab_im, bb_re, bb_im = disc
    bsz, L, _ = u.shape
    uf = u.astype(jnp.float32).reshape(bsz, L, SSM_GROUPS, SSM_CH)
    bu_re = jnp.einsum('gpc,blgc->blgp', bb_re, uf)
    bu_im = jnp.einsum('gpc,blgc->blgp', bb_im, uf)
    h0r = h0_re.astype(jnp.float32)
    h0i = h0_im.astype(jnp.float32)
    bu_re = bu_re.at[:, 0].add(ab_re * h0r - ab_im * h0i)
    bu_im = bu_im.at[:, 0].add(ab_re * h0i + ab_im * h0r)
    a_r = jnp.broadcast_to(ab_re, bu_re.shape)
    a_i = jnp.broadcast_to(ab_im, bu_im.shape)
    _, _, h_re, h_im = lax.associative_scan(_cscan_combine, (a_r, a_i, bu_re, bu_im), axis=1)
    y = (jnp.einsum('gcp,blgp->blgc', c_re.astype(jnp.float32), h_re)
         - jnp.einsum('gcp,blgp->blgc', c_im.astype(jnp.float32), h_im))
    y = y.reshape(bsz, L, D_SSM) + d_skip * u
    z = jax.nn.gelu(y)
    out = z * jax.nn.sigmoid(z @ w_glu + b_glu)
    return out.astype(u.dtype), h_re[:, -1], h_im[:, -1]


def compress_blocks(rows, pe, w1, w2):
    lead = rows.shape[:-3]
    t = rows.shape[-3]
    xb = rows.reshape(lead + (t // BLOCK, BLOCK, N_KV_HEADS, HEAD_DIM)) + pe[:, None, :]
    hid = jax.nn.gelu(jnp.einsum('...nsgd,sdf->...ngf', xb, w1))
    return jnp.einsum('...ngf,fd->...ngd', hid, w2)


def nsa_attend(q, qpos, kc, vc, ks_blk, vs_blk, kw, vw, wpos, gates):
    lq = q.shape[0]
    nb = kc.shape[0]
    qg = (q.astype(jnp.float32) * HEAD_DIM ** -0.5).reshape(lq, N_KV_HEADS, GQ, HEAD_DIM)
    blk = jnp.arange(nb)
    qblk = qpos // BLOCK
    m_c = (blk[None, :] * BLOCK + (BLOCK - 1)) <= qpos[:, None]
    s_c = jnp.einsum('qgrd,ngd->qgrn', qg, kc.astype(jnp.float32))
    p_c = jax.nn.softmax(jnp.where(m_c[:, None, None, :], s_c, NEG), axis=-1)
    p_c = p_c * m_c[:, None, None, :].astype(jnp.float32)
    o_c = jnp.einsum('qgrn,ngd->qgrd', p_c, vc.astype(jnp.float32))
    imp = jnp.sum(p_c, axis=2)
    forced = (blk[None, :] == 0) | (blk[None, :] == qblk[:, None]) | (blk[None, :] == qblk[:, None] - 1)
    causal_b = blk[None, :] <= qblk[:, None]
    score = jnp.where(causal_b[:, None, :], imp + jnp.where(forced, FORCE_BONUS, 0.0)[:, None, :], NEG)
    n_sel = min(N_SEL, nb)
    _, idx = lax.top_k(score, n_sel)
    head_ix = jnp.arange(N_KV_HEADS)[None, :, None]
    k_sel = ks_blk[head_ix, idx].astype(jnp.float32)
    v_sel = vs_blk[head_ix, idx].astype(jnp.float32)
    kpos = idx[..., None] * BLOCK + jnp.arange(BLOCK)
    m_s = (idx <= qblk[:, None, None])[..., None] & (kpos <= qpos[:, None, None, None])
    s_s = jnp.einsum('qgrd,qgnbd->qgrnb', qg, k_sel)
    s_s = jnp.where(m_s[:, :, None], s_s, NEG).reshape(lq, N_KV_HEADS, GQ, n_sel * BLOCK)
    p_s = jax.nn.softmax(s_s, axis=-1).reshape(lq, N_KV_HEADS, GQ, n_sel, BLOCK)
    o_s = jnp.einsum('qgrnb,qgnbd->qgrd', p_s, v_sel)
    m_w = (wpos[None, :] <= qpos[:, None]) & (wpos[None, :] >= qpos[:, None] - WINDOW) & (wpos[None, :] >= 0)
    s_w = jnp.einsum('qgrd,tgd->qgrt', qg, kw.astype(jnp.float32))
    p_w = jax.nn.softmax(jnp.where(m_w[:, None, None, :], s_w, NEG), axis=-1)
    o_w = jnp.einsum('qgrt,tgd->qgrd', p_w, vw.astype(jnp.float32))
    g = gates.astype(jnp.float32).reshape(lq, N_KV_HEADS, GQ, 3)
    o = g[..., 0:1] * o_c + g[..., 1:2] * o_s + g[..., 2:3] * o_w
    return o.reshape(lq, N_HEADS * HEAD_DIM)


def hier_moe(h, w_rg, b_rg, w_rexp, b_rexp, w_e1, w_e3, w_e2):
    shape = h.shape
    t = h.reshape(-1, shape[-1])
    n = t.shape[0]
    rows = jnp.arange(n)
    g_logits = (t @ w_rg + b_rg).astype(jnp.float32)
    g_sel = jnp.argmax(g_logits, axis=-1)
    p_grp = jax.nn.softmax(g_logits, axis=-1)[rows, g_sel]
    e_logits = (t @ w_rexp + b_rexp).astype(jnp.float32).reshape(n, N_EXPERT_GROUPS, EXPERTS_PER_GROUP)[rows, g_sel]
    top_p, top_i = lax.top_k(jax.nn.softmax(e_logits, axis=-1), TOP_K_INNER)
    w_tok = p_grp[:, None] * top_p / jnp.sum(top_p, axis=-1, keepdims=True)
    e_idx = g_sel[:, None] * EXPERTS_PER_GROUP + top_i
    gate = jnp.einsum('nk,nke->ne', w_tok, jax.nn.one_hot(e_idx, N_EXPERTS, dtype=jnp.float32))
    hid = jax.nn.silu(jnp.einsum('nd,edf->nef', t, w_e1)) * jnp.einsum('nd,edf->nef', t, w_e3)
    y = jnp.einsum('nef,efd->nd', hid * gate[:, :, None].astype(hid.dtype), w_e2)
    return y.reshape(shape)


def layer_step(x_p, x_s, c_p, c_s, cache_kv, cache_win, st_re, st_im, page_table,
               w_ada, b_ada, g_norm1, g_norm2, w_in, g_q, g_kc, g_ks, g_kw,
               pe_k, pe_v, w_ck1, w_ck2, w_cv1, w_cv2,
               ssm_a_re, ssm_a_im, ssm_log_dt, ssm_b_re, ssm_b_im, ssm_c_re, ssm_c_im, ssm_d,
               w_glu, b_glu, g_out_ssm, g_out_attn, w_out,
               w_rg, b_rg, w_rexp, b_rexp, w_e1, w_e3, w_e2):
    disc = s5_discretize(ssm_a_re, ssm_a_im, ssm_log_dt, ssm_b_re, ssm_b_im)

    def in_proj(h):
        b, L = h.shape[:2]
        p = h @ w_in
        o1 = D_SSM
        o2 = o1 + N_HEADS * HEAD_DIM
        o3 = o2 + 6 * N_KV_HEADS * HEAD_DIM
        u = p[..., :o1]
        q = rms_norm(p[..., o1:o2].reshape(b, L, N_HEADS, HEAD_DIM), g_q)
        kv = p[..., o2:o3].reshape(b, L, 6, N_KV_HEADS, HEAD_DIM)
        gates = jax.nn.sigmoid(p[..., o3:].reshape(b, L, N_HEADS, 3))
        kv_rows = jnp.stack([kv[:, :, 0], kv[:, :, 1], rms_norm(kv[:, :, 2], g_ks), kv[:, :, 3]], axis=2)
        win_rows = jnp.stack([rms_norm(kv[:, :, 4], g_kw), kv[:, :, 5]], axis=2)
        return u, q, kv_rows, win_rows, gates

    def compress_kv(rows):
        kc = rms_norm(compress_blocks(rows[..., 0, :, :], pe_k, w_ck1, w_ck2), g_kc)
        vc = compress_blocks(rows[..., 1, :, :], pe_v, w_cv1, w_cv2)
        return kc, vc

    def nsa_prompt(q, kv_rows, win_rows, gates):
        b, L = q.shape[:2]
        nb = L // BLOCK
        kc, vc = compress_kv(kv_rows)
        ks_blk = kv_rows[:, :, 2].reshape(b, nb, BLOCK, N_KV_HEADS, HEAD_DIM).transpose(0, 3, 1, 2, 4)
        vs_blk = kv_rows[:, :, 3].reshape(b, nb, BLOCK, N_KV_HEADS, HEAD_DIM).transpose(0, 3, 1, 2, 4)
        w_pad = jnp.pad(win_rows, ((0, 0), (WINDOW, 0), (0, 0), (0, 0), (0, 0)))
        attend = jax.vmap(nsa_attend, in_axes=(0, None, 0, 0, 0, 0, 0, 0, None, 0))

        def one_block(i):
            s = i * Q_BLOCK
            qpos = s + jnp.arange(Q_BLOCK)
            wpos = s - WINDOW + jnp.arange(WINDOW + Q_BLOCK)
            qc = lax.dynamic_slice_in_dim(q, s, Q_BLOCK, axis=1)
            gc = lax.dynamic_slice_in_dim(gates, s, Q_BLOCK, axis=1)
            wc = lax.dynamic_slice_in_dim(w_pad, s, WINDOW + Q_BLOCK, axis=1)
            return attend(qc, qpos, kc, vc, ks_blk, vs_blk, wc[:, :, 0], wc[:, :, 1], wpos, gc)

        o = lax.map(one_block, jnp.arange(L // Q_BLOCK))
        return o.transpose(1, 0, 2, 3).reshape(b, L, N_HEADS * HEAD_DIM)

    def nsa_sample(q, kv_rows, win_rows, gates):
        s_new = q.shape[1]
        past = page_table.shape[1] * PAGE_SIZE
        total = past + s_new
        nb = -(-total // BLOCK)
        wbuf = cache_win.shape[1]
        qpos = past + jnp.arange(s_new)
        wpos = past - wbuf + jnp.arange(wbuf + s_new)

        def one_seq(args):
            qs, kvs, ws, gs, pt, wb = args
            past_rows = cache_kv[pt].reshape(past, N_KV_SLOTS, N_KV_HEADS, HEAD_DIM)
            rows = jnp.concatenate([past_rows.astype(kvs.dtype), kvs], axis=0)
            rows = jnp.pad(rows, ((0, nb * BLOCK - total), (0, 0), (0, 0), (0, 0)))
            kc, vc = compress_kv(rows)
            ks_blk = rows[:, 2].reshape(nb, BLOCK, N_KV_HEADS, HEAD_DIM).transpose(2, 0, 1, 3)
            vs_blk = rows[:, 3].reshape(nb, BLOCK, N_KV_HEADS, HEAD_DIM).transpose(2, 0, 1, 3)
            wrows = jnp.concatenate([wb.astype(ws.dtype), ws], axis=0)
            return nsa_attend(qs, qpos, kc, vc, ks_blk, vs_blk, wrows[:, 0], wrows[:, 1], wpos, gs)

        return lax.map(one_seq, (q, kv_rows, win_rows, gates, page_table, cache_win))

    def block(x, c, h0_re, h0_im, attend_fn):
        sh1, sc1, ga1, sh2, sc2, ga2 = ada_terms(c, w_ada, b_ada)
        u, q, kv_rows, win_rows, gates = in_proj(modulate(x, g_norm1, sh1, sc1))
        ssm_out, h_re, h_im = s5_mixer(u, h0_re, h0_im, disc, ssm_c_re, ssm_c_im, ssm_d, w_glu, b_glu)
        attn_out = attend_fn(q, kv_rows, win_rows, gates)
        merged = jnp.concatenate([rms_norm(ssm_out, g_out_ssm), rms_norm(attn_out, g_out_attn)], axis=-1) @ w_out
        x = x + ga1 * merged
        x = x + ga2 * hier_moe(modulate(x, g_norm2, sh2, sc2), w_rg, b_rg, w_rexp, b_rexp, w_e1, w_e3, w_e2)
        return x, kv_rows, win_rows, h_re, h_im

    zeros = jnp.zeros((x_p.shape[0], SSM_GROUPS, SSM_STATE), jnp.float32)
    y_p, kv_p, win_all_p, re_p, im_p = block(x_p, c_p, zeros, zeros, nsa_prompt)
    win_p = win_all_p[:, -min(WINDOW, x_p.shape[1]):]
    y_s, kv_s, win_new, re_s, im_s = block(x_s, c_s, st_re, st_im, nsa_sample)
    wbuf = cache_win.shape[1]
    win_s = jnp.concatenate([cache_win.astype(win_new.dtype), win_new], axis=1)[:, -wbuf:]
    return (y_p, y_s, kv_p, win_p, re_p, im_p, kv_s, win_s, re_s, im_s)


def setup_inputs(seed: int = 0) -> dict:
    key = jax.random.key(seed)
    keys = iter(jax.random.split(key, 64))

    def nrm(shape, scale=1.0):
        return scale * jax.random.normal(next(keys), shape, jnp.float32)

    def gain(shape):
        return 1.0 + nrm(shape, 0.02)

    L = DEPTH
    n_pages = PAST_LEN // PAGE_SIZE
    n_used = DEC_BATCH * n_pages
    n_pool = n_used + max(1, n_used // 4)
    wbuf = min(WINDOW, PAST_LEN)
    d_attn = N_HEADS * HEAD_DIM
    page_table = jax.random.permutation(next(keys), n_pool)[:n_used].reshape(DEC_BATCH, n_pages).astype(jnp.int32)
    log_dt = jax.random.uniform(next(keys), (L, SSM_GROUPS), jnp.float32, math.log(DT_MIN), math.log(DT_MAX))
    a_im = math.pi * jnp.arange(SSM_STATE, dtype=jnp.float32) + nrm((L, SSM_GROUPS, SSM_STATE), 0.01)
    return {
        'x_prompt': nrm((BATCH, SEQ, D_MODEL)),
        'x_sample': nrm((DEC_BATCH, DEC_SEQ, D_MODEL)),
        'c_prompt': nrm((BATCH, D_MODEL)),
        'c_sample': nrm((DEC_BATCH, D_MODEL)),
        'cache_kv': nrm((L, n_pool, PAGE_SIZE, N_KV_SLOTS, N_KV_HEADS, HEAD_DIM)),
        'cache_win': nrm((L, DEC_BATCH, wbuf, 2, N_KV_HEADS, HEAD_DIM)),
        'state_ssm_re': nrm((L, DEC_BATCH, SSM_GROUPS, SSM_STATE), 0.3),
        'state_ssm_im': nrm((L, DEC_BATCH, SSM_GROUPS, SSM_STATE), 0.3),
        'page_table': page_table,
        'w_ada': nrm((L, D_MODEL, 6 * D_MODEL), 0.2 * D_MODEL ** -0.5),
        'b_ada': nrm((L, 6 * D_MODEL), 0.02),
        'g_norm1': gain((L, D_MODEL)),
        'g_norm2': gain((L, D_MODEL)),
        'w_in': nrm((L, D_MODEL, D_IN), D_MODEL ** -0.5),
        'g_q': gain((L, HEAD_DIM)),
        'g_kc': gain((L, HEAD_DIM)),
        'g_ks': gain((L, HEAD_DIM)),
        'g_kw': gain((L, HEAD_DIM)),
        'pe_k': nrm((L, BLOCK, HEAD_DIM), 0.1),
        'pe_v': nrm((L, BLOCK, HEAD_DIM), 0.1),
        'w_ck1': nrm((L, BLOCK, HEAD_DIM, CMP_HIDDEN), (BLOCK * HEAD_DIM) ** -0.5),
        'w_ck2': nrm((L, CMP_HIDDEN, HEAD_DIM), CMP_HIDDEN ** -0.5),
        'w_cv1': nrm((L, BLOCK, HEAD_DIM, CMP_HIDDEN), (BLOCK * HEAD_DIM) ** -0.5),
        'w_cv2': nrm((L, CMP_HIDDEN, HEAD_DIM), CMP_HIDDEN ** -0.5),
        'ssm_a_re': -0.5 + nrm((L, SSM_GROUPS, SSM_STATE), 0.01),
        'ssm_a_im': a_im,
        'ssm_log_dt': log_dt,
        'ssm_b_re': nrm((L, SSM_GROUPS, SSM_STATE, SSM_CH), (2 * SSM_CH) ** -0.5),
        'ssm_b_im': nrm((L, SSM_GROUPS, SSM_STATE, SSM_CH), (2 * SSM_CH) ** -0.5),
        'ssm_c_re': nrm((L, SSM_GROUPS, SSM_CH, SSM_STATE), (2 * SSM_STATE) ** -0.5),
        'ssm_c_im': nrm((L, SSM_GROUPS, SSM_CH, SSM_STATE), (2 * SSM_STATE) ** -0.5),
        'ssm_d': nrm((L, D_SSM)),
        'w_glu': nrm((L, D_SSM, D_SSM), D_SSM ** -0.5),
        'b_glu': nrm((L, D_SSM), 0.02),
        'g_out_ssm': gain((L, D_SSM)),
        'g_out_attn': gain((L, d_attn)),
        'w_out': nrm((L, D_SSM + d_attn, D_MODEL), (D_SSM + d_attn) ** -0.5),
        'w_rg': nrm((L, D_MODEL, N_EXPERT_GROUPS), D_MODEL ** -0.5),
        'b_rg': nrm((L, N_EXPERT_GROUPS), 0.01),
        'w_rexp': nrm((L, D_MODEL, N_EXPERTS), D_MODEL ** -0.5),
        'b_rexp': nrm((L, N_EXPERTS), 0.01),
        'w_e1': nrm((L, N_EXPERTS, D_MODEL, D_EXPERT), D_MODEL ** -0.5),
        'w_e3': nrm((L, N_EXPERTS, D_MODEL, D_EXPERT), D_MODEL ** -0.5),
        'w_e2': nrm((L, N_EXPERTS, D_EXPERT, D_MODEL), D_EXPERT ** -0.5),
    }


def reference(x_prompt, x_sample, c_prompt, c_sample, cache_kv, cache_win, state_ssm_re, state_ssm_im, page_table,
              w_ada, b_ada, g_norm1, g_norm2, w_in, g_q, g_kc, g_ks, g_kw,
              pe_k, pe_v, w_ck1, w_ck2, w_cv1, w_cv2,
              ssm_a_re, ssm_a_im, ssm_log_dt, ssm_b_re, ssm_b_im, ssm_c_re, ssm_c_im, ssm_d,
              w_glu, b_glu, g_out_ssm, g_out_attn, w_out,
              w_rg, b_rg, w_rexp, b_rexp, w_e1, w_e3, w_e2):
    y_p = x_prompt
    y_s = x_sample
    new_lists = [[] for _ in range(8)]
    for l in range(DEPTH):
        y_p, y_s, *new = layer_step(
            y_p, y_s, c_prompt, c_sample, cache_kv[l], cache_win[l], state_ssm_re[l], state_ssm_im[l], page_table,
            w_ada[l], b_ada[l], g_norm1[l], g_norm2[l], w_in[l], g_q[l], g_kc[l], g_ks[l], g_kw[l],
            pe_k[l], pe_v[l], w_ck1[l], w_ck2[l], w_cv1[l], w_cv2[l],
            ssm_a_re[l], ssm_a_im[l], ssm_log_dt[l], ssm_b_re[l], ssm_b_im[l], ssm_c_re[l], ssm_c_im[l], ssm_d[l],
            w_glu[l], b_glu[l], g_out_ssm[l], g_out_attn[l], w_out[l],
            w_rg[l], b_rg[l], w_rexp[l], b_rexp[l], w_e1[l], w_e3[l], w_e2[l])
        for lst, arr in zip(new_lists, new):
            lst.append(arr)
    kv_p, win_p, re_p, im_p, kv_s, win_s, re_s, im_s = [jnp.stack(v) for v in new_lists]
    return (y_p, y_s, kv_p, win_p, re_p, im_p, kv_s, win_s, re_s, im_s)
```

```python
import functools
import math

import jax
import jax.numpy as jnp
from jax import lax
from jax.experimental import pallas as pl
from jax.experimental.pallas import tpu as pltpu

F32 = jnp.float32
BF16 = jnp.bfloat16
I32 = jnp.int32
HIGHEST = lax.Precision.HIGHEST

EPS = 1e-6
NEG = -1e30
FORCE_BONUS = 1e4

LANE = 128
HEAD_DIM = 128
N_KV = 4
GQ = 4
KV_W = N_KV * HEAD_DIM
BLOCK = 64
N_SEL = 16
WINDOW = 512
Q_TILE = 128
KEY_TILE = 512
SSM_CH = 16
SSM_STATE = 64
S5_CHUNK_CH = 128
S5_CHUNK_ST = S5_CHUNK_CH // SSM_CH * SSM_STATE
N_GROUPS = 4
EXPERTS_PER_GROUP = 8
N_EXPERTS = N_GROUPS * EXPERTS_PER_GROUP
PROJ_TN = 512
MOE_TM = 256


def _cparams(sem, vmem_mb):
    return pltpu.CompilerParams(dimension_semantics=sem, vmem_limit_bytes=vmem_mb << 20)


def _rms(x, g):
    return x * lax.rsqrt(jnp.mean(x * x, axis=-1, keepdims=True) + EPS) * g


def _ada_body(c_ref, w_ref, b_ref, o_ref):
    c = c_ref[...]
    a = (c * jax.nn.sigmoid(c)).astype(BF16)
    o_ref[...] = jnp.dot(a, w_ref[...].astype(BF16), preferred_element_type=F32) + b_ref[...]


def _ada(c_all, w_ada, b_ada):
    r, d = c_all.shape
    n = w_ada.shape[1]
    tn = 512
    return pl.pallas_call(
        _ada_body,
        grid=(n // tn,),
        in_specs=[pl.BlockSpec((r, d), lambda j: (0, 0)),
                  pl.BlockSpec((d, tn), lambda j: (0, j)),
                  pl.BlockSpec((1, tn), lambda j: (0, j))],
        out_specs=pl.BlockSpec((r, tn), lambda j: (0, j)),
        out_shape=jax.ShapeDtypeStruct((r, n), F32),
        compiler_params=_cparams(("arbitrary",), 40),
        name="ada",
    )(c_all, w_ada, b_ada.reshape(1, n))


def _head_norm(acc, g):
    return jnp.concatenate(
        [_rms(acc[:, h * HEAD_DIM:(h + 1) * HEAD_DIM], g) for h in range(N_KV)], axis=-1)


def _inproj_body(x_ref, sh_ref, sc_ref, g1_ref, w_ref, gq_ref, gks_ref, gkw_ref,
                 u_ref, q_ref, kv_ref, win_ref, gt_ref, kvb_ref, winb_ref, cmp_ref, h_scr, *, nu):
    j = pl.program_id(1)

    @pl.when(j == 0)
    def _():
        h = _rms(x_ref[...], g1_ref[...]) * (1.0 + sc_ref[...]) + sh_ref[...]
        h_scr[...] = h.astype(BF16)

    acc = jnp.dot(h_scr[...], w_ref[...], preferred_element_type=F32)

    @pl.when(j < nu)
    def _():
        u_ref[...] = acc

    @pl.when((j >= nu) & (j < 2 * nu))
    def _():
        q_ref[...] = (_head_norm(acc, gq_ref[...]) * (HEAD_DIM ** -0.5)).astype(BF16)

    jk = j - 2 * nu

    @pl.when((jk >= 0) & (jk < 4) & (jk != 2))
    def _():
        kv_ref[...] = acc
        kvb_ref[...] = acc.astype(BF16)

    @pl.when((jk == 0) | (jk == 1))
    def _():
        for h in range(N_KV):
            cmp_ref[h] = acc[:, h * HEAD_DIM:(h + 1) * HEAD_DIM]

    @pl.when(jk == 2)
    def _():
        kn = _head_norm(acc, gks_ref[...])
        kv_ref[...] = kn
        kvb_ref[...] = kn.astype(BF16)

    @pl.when(jk == 4)
    def _():
        kn = _head_norm(acc, gkw_ref[...])
        win_ref[...] = kn
        winb_ref[...] = kn.astype(BF16)

    @pl.when(jk == 5)
    def _():
        win_ref[...] = acc
        winb_ref[...] = acc.astype(BF16)

    @pl.when(jk == 6)
    def _():
        gt_ref[...] = jax.nn.sigmoid(acc)


def _inproj(x2d, x_spec, sh, sc, mod_spec, n_rows, tm, g1, w_in_p, g_q, g_ks, g_kw, d_ssm):
    d = g1.shape[-1]
    nu = d_ssm // PROJ_TN
    nj = w_in_p.shape[1] // PROJ_TN
    assert nj == 2 * nu + 7
    ni = n_rows // tm
    clip = lambda v, lo, hi: jnp.minimum(jnp.maximum(v, lo), hi)
    vec = lambda n: pl.BlockSpec((1, n), lambda i, j: (0, 0))
    outs = [
        (jax.ShapeDtypeStruct((n_rows, d_ssm), F32), lambda i, j: (i, jnp.minimum(j, nu - 1))),
        (jax.ShapeDtypeStruct((n_rows, d_ssm), BF16), lambda i, j: (i, clip(j - nu, 0, nu - 1))),
        (jax.ShapeDtypeStruct((n_rows, 4 * KV_W), F32), lambda i, j: (i, clip(j - 2 * nu, 0, 3))),
        (jax.ShapeDtypeStruct((n_rows, 2 * KV_W), F32), lambda i, j: (i, clip(j - 2 * nu - 4, 0, 1))),
        (jax.ShapeDtypeStruct((n_rows, PROJ_TN), F32), lambda i, j: (i, 0)),
        (jax.ShapeDtypeStruct((n_rows, 4 * KV_W), BF16), lambda i, j: (i, clip(j - 2 * nu, 0, 3))),
        (jax.ShapeDtypeStruct((n_rows, 2 * KV_W), BF16), lambda i, j: (i, clip(j - 2 * nu - 4, 0, 1))),
    ]
    cmp_shape = jax.ShapeDtypeStruct((2 * N_KV, n_rows, HEAD_DIM), F32)
    cmp_spec = pl.BlockSpec((N_KV, tm, HEAD_DIM), lambda i, j: (clip(j - 2 * nu, 0, 1), i, 0))
    return pl.pallas_call(
        functools.partial(_inproj_body, nu=nu),
        grid=(ni, nj),
        in_specs=[x_spec, mod_spec, mod_spec, vec(d),
                  pl.BlockSpec((d, PROJ_TN), lambda i, j: (0, j)),
                  vec(HEAD_DIM), vec(HEAD_DIM), vec(HEAD_DIM)],
        out_specs=[pl.BlockSpec((tm, PROJ_TN), im) for _, im in outs] + [cmp_spec],
        out_shape=[s for s, _ in outs] + [cmp_shape],
        scratch_shapes=[pltpu.VMEM((tm, d), BF16)],
        compiler_params=_cparams(("arbitrary", "arbitrary"), 56),
        name="inproj",
    )(x2d, sh, sc, g1, w_in_p, g_q, g_ks, g_kw)


def _disc_body(are_ref, aim_ref, ldt_ref, br_ref, bi_ref, abr_ref, abi_ref, bbr_ref, bbi_ref):
    lam_re = jnp.minimum(are_ref[...], -1e-4)
    lam_im = aim_ref[...]
    dt = jnp.exp(ldt_ref[...])
    mag = jnp.exp(lam_re * dt)
    ab_re = mag * jnp.cos(lam_im * dt)
    ab_im = mag * jnp.sin(lam_im * dt)
    den = lam_re * lam_re + lam_im * lam_im
    n_re = ab_re - 1.0
    f_re = (n_re * lam_re + ab_im * lam_im) / den
    f_im = (ab_im * lam_re - n_re * lam_im) / den
    abr_ref[...] = ab_re
    abi_ref[...] = ab_im
    br = br_ref[...]
    bi = bi_ref[...]
    bbr_ref[...] = f_re[None] * br - f_im[None] * bi
    bbi_ref[...] = f_re[None] * bi + f_im[None] * br


def _discretize(a_re, a_im, log_dt, b_re, b_im):
    g, p = a_re.shape
    c = b_re.shape[0]
    gp = jax.ShapeDtypeStruct((g, p), F32)
    cgp = jax.ShapeDtypeStruct((c, g, p), F32)
    return pl.pallas_call(_disc_body, out_shape=[gp, gp, cgp, cgp], name="s5_disc")(
        a_re, a_im, log_dt.reshape(g, 1), b_re, b_im)


def _cscan_step(ar, ai, hr, hi, br, bi):
    return ar * hr - ai * hi + br, ar * hi + ai * hr + bi


def _s5_prompt_body(u_ref, bd_ref, cd_ref, are_ref, aim_ref, d_ref, h0r_ref, h0i_ref,
                    z_ref, sre_ref, sim_ref,
                    lhs_scr, bu_scr, y_scr, end_scr, init_scr, *, seq, nseg, ic):
    seg = seq // nseg
    nst = S5_CHUNK_ST
    half = min(nst, 512)
    nparts = nst // half
    rows_c = ic * nseg
    n_chunks = seg // ic

    def perm(i, _):
        lhs_scr[pl.ds(pl.multiple_of(i * nseg, nseg), nseg), :] = u_ref[pl.ds(i, nseg, stride=seg), :]
        return 0
    lax.fori_loop(0, seg, perm, 0)

    a_re = are_ref[...]
    a_im = aim_ref[...]

    def scan_chunk(ci, carry, store):
        r0 = pl.multiple_of(ci * rows_c, rows_c)
        lhs = lhs_scr[pl.ds(r0, rows_c), :].astype(BF16)
        bu_scr[...] = jnp.dot(lhs, bd_ref[...], preferred_element_type=F32)
        out = []
        for hf in range(nparts):
            lo = hf * half
            ar = jnp.broadcast_to(a_re[:, lo:lo + half], (nseg, half))
            ai = jnp.broadcast_to(a_im[:, lo:lo + half], (nseg, half))

            def step(i, c, lo=lo, ar=ar, ai=ai):
                rr = pl.multiple_of(i * nseg, nseg)
                br = bu_scr[pl.ds(rr, nseg), lo:lo + half]
                bi = bu_scr[pl.ds(rr, nseg), nst + lo:nst + lo + half]
                nr, ni = _cscan_step(ar, ai, c[0], c[1], br, bi)
                if store:
                    bu_scr[pl.ds(rr, nseg), lo:lo + half] = nr
                    bu_scr[pl.ds(rr, nseg), nst + lo:nst + lo + half] = ni
                return nr, ni
            out.append(lax.fori_loop(0, ic, step, carry[hf]))
        return tuple(out)

    row0 = lax.broadcasted_iota(I32, (nseg, half), 0) == 0
    h0r = h0r_ref[...]
    h0i = h0i_ref[...]
    init1 = tuple((jnp.where(row0, h0r[:, hf * half:(hf + 1) * half], 0.0),
                   jnp.where(row0, h0i[:, hf * half:(hf + 1) * half], 0.0)) for hf in range(nparts))
    ends = lax.fori_loop(0, n_chunks, lambda ci, c: scan_chunk(ci, c, False), init1)
    for hf in range(nparts):
        end_scr[:, hf * half:(hf + 1) * half] = ends[hf][0]
        end_scr[:, nst + hf * half:nst + (hf + 1) * half] = ends[hf][1]

    pr, pi_ = a_re, a_im
    for _ in range(int(math.log2(seg))):
        pr, pi_ = pr * pr - pi_ * pi_, 2.0 * pr * pi_
    t_re, t_im = h0r, h0i
    for j in range(nseg):
        init_scr[j:j + 1, 0:nst] = t_re
        init_scr[j:j + 1, nst:2 * nst] = t_im
        e_re = end_scr[j:j + 1, 0:nst]
        e_im = end_scr[j:j + 1, nst:2 * nst]
        if j == 0:
            t_re, t_im = e_re, e_im
        else:
            t_re, t_im = e_re + pr * t_re - pi_ * t_im, e_im + pr * t_im + pi_ * t_re
    sre_ref[...] = t_re
    sim_ref[...] = t_im

    init2 = tuple((init_scr[:, hf * half:(hf + 1) * half],
                   init_scr[:, nst + hf * half:nst + (hf + 1) * half]) for hf in range(nparts))

    def chunk2(ci, carry):
        carry = scan_chunk(ci, carry, True)
        y_scr[...] = jnp.dot(bu_scr[...].astype(BF16), cd_ref[...], preferred_element_type=F32)

        def unperm(i, _):
            rows = y_scr[pl.ds(pl.multiple_of(i * nseg, nseg), nseg), :]
            z_ref[pl.ds(ci * ic + i, nseg, stride=seg), :] = rows
            return 0
        lax.fori_loop(0, ic, unperm, 0)
        return carry
    lax.fori_loop(0, n_chunks, chunk2, init2)

    d = d_ref[...]

    def fin(k, _):
        r0 = pl.multiple_of(k * 512, 512)
        z_ref[pl.ds(r0, 512), :] = jax.nn.gelu(z_ref[pl.ds(r0, 512), :] + d * u_ref[pl.ds(r0, 512), :])
        return 0
    lax.fori_loop(0, seq // 512, fin, 0)


def _s5_prompt(u, bd, cd, ab_re, ab_im, d_skip, h0_re, h0_im, batch, seq):
    nck = bd.shape[0]
    nseg, ic = 8, 64
    assert seq % (nseg * ic) == 0 and (seq // nseg) & (seq // nseg - 1) == 0
    nst = S5_CHUNK_ST
    st = jax.ShapeDtypeStruct((batch, 1, nck * nst), F32)
    st_spec = pl.BlockSpec((None, 1, nst), lambda b, c: (b, 0, c))
    return pl.pallas_call(
        functools.partial(_s5_prompt_body, seq=seq, nseg=nseg, ic=ic),
        grid=(batch, nck),
        in_specs=[pl.BlockSpec((seq, S5_CHUNK_CH), lambda b, c: (b, c)),
                  pl.BlockSpec((None, S5_CHUNK_CH, 2 * nst), lambda b, c: (c, 0, 0)),
                  pl.BlockSpec((None, 2 * nst, S5_CHUNK_CH), lambda b, c: (c, 0, 0)),
                  pl.BlockSpec((None, 1, nst), lambda b, c: (c, 0, 0)),
                  pl.BlockSpec((None, 1, nst), lambda b, c: (c, 0, 0)),
                  pl.BlockSpec((1, S5_CHUNK_CH), lambda b, c: (0, c)),
                  st_spec, st_spec],
        out_specs=[pl.BlockSpec((seq, S5_CHUNK_CH), lambda b, c: (b, c)), st_spec, st_spec],
        out_shape=[jax.ShapeDtypeStruct((batch * seq, nck * S5_CHUNK_CH), F32), st, st],
        scratch_shapes=[pltpu.VMEM((seq, S5_CHUNK_CH), F32),
                        pltpu.VMEM((ic * nseg, 2 * nst), F32),
                        pltpu.VMEM((ic * nseg, S5_CHUNK_CH), F32),
                        pltpu.VMEM((nseg, 2 * nst), F32),
                        pltpu.VMEM((nseg, 2 * nst), F32)],
        compiler_params=_cparams(("arbitrary", "arbitrary"), 48),
        name="s5_prompt",
    )(u, bd, cd, ab_re, ab_im, d_skip, h0_re, h0_im)


def _s5_sample_body(u_ref, bd_ref, cd_ref, are_ref, aim_ref, d_ref, h0r_ref, h0i_ref,
                    z_ref, sre_ref, sim_ref, bu_scr, *, nb, nt):
    nst = S5_CHUNK_ST
    bu_scr[...] = jnp.dot(u_ref[...], bd_ref[...], preferred_element_type=F32, precision=HIGHEST)
    ar = are_ref[...]
    ai = aim_ref[...]
    hr = h0r_ref[...]
    hi = h0i_ref[...]
    for t in range(nt):
        rows = slice(t * nb, (t + 1) * nb)
        hr, hi = _cscan_step(ar, ai, hr, hi, bu_scr[rows, 0:nst], bu_scr[rows, nst:2 * nst])
        bu_scr[rows, 0:nst] = hr
        bu_scr[rows, nst:2 * nst] = hi
    sre_ref[...] = hr
    sim_ref[...] = hi
    y = jnp.dot(bu_scr[...].astype(BF16), cd_ref[...], preferred_element_type=F32)
    z_ref[...] = jax.nn.gelu(y + d_ref[...] * u_ref[...])


def _s5_sample(u, bd32, cd, ab_re, ab_im, d_skip, h0_re, h0_im, nb, nt):
    nck = bd32.shape[0]
    nst = S5_CHUNK_ST
    st = jax.ShapeDtypeStruct((nb, nck * nst), F32)
    st_spec = pl.BlockSpec((nb, nst), lambda c: (0, c))
    return pl.pallas_call(
        functools.partial(_s5_sample_body, nb=nb, nt=nt),
        grid=(nck,),
        in_specs=[pl.BlockSpec((nb * nt, S5_CHUNK_CH), lambda c: (0, c)),
                  pl.BlockSpec((None, S5_CHUNK_CH, 2 * nst), lambda c: (c, 0, 0)),
                  pl.BlockSpec((None, 2 * nst, S5_CHUNK_CH), lambda c: (c, 0, 0)),
                  pl.BlockSpec((None, 1, nst), lambda c: (c, 0, 0)),
                  pl.BlockSpec((None, 1, nst), lambda c: (c, 0, 0)),
                  pl.BlockSpec((1, S5_CHUNK_CH), lambda c: (0, c)),
                  st_spec, st_spec],
        out_specs=[pl.BlockSpec((nb * nt, S5_CHUNK_CH), lambda c: (0, c)), st_spec, st_spec],
        out_shape=[jax.ShapeDtypeStruct((nb * nt, nck * S5_CHUNK_CH), F32), st, st],
        scratch_shapes=[pltpu.VMEM((nb * nt, 2 * nst), F32)],
        compiler_params=_cparams(("arbitrary",), 40),
        name="s5_sample",
    )(u, bd32, cd, ab_re, ab_im, d_skip, h0_re, h0_im)


def _s5_weights(ab_re, ab_im, bb_re, bb_im, c_re, c_im):
    g, p = ab_re.shape
    gl = S5_CHUNK_CH // SSM_CH
    nck = g // gl
    eye = jnp.eye(gl, dtype=F32)

    def bdiag(bb):
        x = bb.reshape(SSM_CH, nck, gl, p).transpose(1, 2, 0, 3)
        return jnp.einsum('kgcp,gh->kgchp', x, eye).reshape(nck, gl * SSM_CH, gl * p)

    def cdiag(cc):
        x = cc.reshape(nck, gl, SSM_CH, p)
        return jnp.einsum('kgcp,gh->kgphc', x, eye).reshape(nck, gl * p, gl * SSM_CH)

    bd = jnp.concatenate([bdiag(bb_re), bdiag(bb_im)], axis=-1)
    cd = jnp.concatenate([cdiag(c_re), -cdiag(c_im)], axis=1)
    return bd, cd.astype(BF16), ab_re.reshape(nck, 1, gl * p), ab_im.reshape(nck, 1, gl * p)


def _glu_body(z_ref, w_ref, b_ref, g_ref, o_ref):
    z = z_ref[...]
    gate = jnp.dot(z.astype(BF16), w_ref[...], preferred_element_type=F32) + b_ref[...]
    o_ref[...] = _rms(z * jax.nn.sigmoid(gate), g_ref[...]).astype(BF16)


def _glu(z, w_glu_b, b_glu, g_out, tm):
    n, d = z.shape
    vec = pl.BlockSpec((1, d), lambda i: (0, 0))
    return pl.pallas_call(
        _glu_body,
        grid=(n // tm,),
        in_specs=[pl.BlockSpec((tm, d), lambda i: (i, 0)),
                  pl.BlockSpec((d, d), lambda i: (0, 0)), vec, vec],
        out_specs=pl.BlockSpec((tm, d), lambda i: (i, 0)),
        out_shape=jax.ShapeDtypeStruct((n, d), BF16),
        compiler_params=_cparams(("arbitrary",), 40),
        name="s5_glu",
    )(z, w_glu_b, b_glu, g_out)


def _compress_rows(pe_ref, w1_ref, nblk, head_slices):
    hidden = w1_ref.shape[-1]

    def pair(sp, acc):
        xs = []
        for ds_ in range(2):
            s = 2 * sp + ds_
            pe = pe_ref[pl.ds(s, 1), :]
            xs.append(jnp.concatenate(
                [hs[pl.ds(s, nblk, stride=BLOCK), :] + pe for hs in head_slices], axis=0))
        x = jnp.concatenate(xs, axis=-1).astype(BF16)
        w = w1_ref[pl.ds(pl.multiple_of(sp * 2 * HEAD_DIM, 2 * HEAD_DIM), 2 * HEAD_DIM), :]
        return acc + jnp.dot(x, w, preferred_element_type=F32)

    acc0 = jnp.zeros((len(head_slices) * nblk, hidden), F32)
    return lax.fori_loop(0, BLOCK // 2, pair, acc0)


def _compress_body(rows_ref, pe_ref, w1_ref, w2_ref, g_ref, o_ref, *, nblk):
    slot = pl.program_id(1)
    heads = [rows_ref.at[h] for h in range(N_KV)]
    hid = jax.nn.gelu(_compress_rows(pe_ref, w1_ref, nblk, heads))
    out = jnp.dot(hid.astype(BF16), w2_ref[...], preferred_element_type=F32)

    @pl.when(slot == 0)
    def _():
        o_ref[...] = _rms(out, g_ref[...]).reshape(N_KV, nblk, HEAD_DIM)

    @pl.when(slot != 0)
    def _():
        o_ref[...] = out.reshape(N_KV, nblk, HEAD_DIM)


def _compress_prompt(kv_heads, pe, w1, w2, g_kc, batch, seq):
    nblk = seq // BLOCK
    hidden = w1.shape[-1]
    return pl.pallas_call(
        functools.partial(_compress_body, nblk=nblk),
        grid=(batch, 2),
        in_specs=[pl.BlockSpec((N_KV, seq, HEAD_DIM), lambda b, s: (s, b, 0)),
                  pl.BlockSpec((None, BLOCK, HEAD_DIM), lambda b, s: (s, 0, 0)),
                  pl.BlockSpec((None, BLOCK * HEAD_DIM, hidden), lambda b, s: (s, 0, 0)),
                  pl.BlockSpec((None, hidden, HEAD_DIM), lambda b, s: (s, 0, 0)),
                  pl.BlockSpec((1, HEAD_DIM), lambda b, s: (0, 0))],
        out_specs=pl.BlockSpec((None, None, N_KV, nblk, HEAD_DIM), lambda b, s: (b, s, 0, 0, 0)),
        out_shape=jax.ShapeDtypeStruct((batch, 2, N_KV, nblk, HEAD_DIM), F32),
        compiler_params=_cparams(("arbitrary", "arbitrary"), 48),
        name="nsa_compress",
    )(kv_heads, pe, w1, w2, g_kc)


def _softmax_rows(s):
    e = jnp.exp(s - jnp.max(s, axis=-1, keepdims=True))
    return e / jnp.sum(e, axis=-1, keepdims=True)


def _select_blocks(imp, blk, qblk, nb_real, axis=0):
    forced = (blk == 0) | (blk == qblk) | (blk == qblk - 1)
    causal = blk <= qblk
    score = jnp.where(causal, imp + jnp.where(forced, FORCE_BONUS, 0.0), NEG)
    cnt = jnp.zeros(score.shape, F32)
    for m in range(nb_real):
        row = score[m:m + 1, :] if axis == 0 else score[:, m:m + 1]
        beats = (row > score) | ((row == score) & (blk > m))
        cnt = cnt + jnp.where(beats, 1.0, 0.0)
    return jnp.where((cnt < N_SEL) & causal, 1.0, 0.0)


def _nsa_prompt_body(q_ref, gt_ref, kc_ref, vc_ref, ks_ref, vs_ref, kw_ref, vw_ref, e_ref,
                     o_ref, m_scr, l_scr, acc_scr, *, seq, nblk, wkeys):
    i = pl.program_id(2)
    qpos0 = i * Q_TILE
    q = q_ref[...]
    qs = [q[:, r * HEAD_DIM:(r + 1) * HEAD_DIM] for r in range(GQ)]
    q_all = jnp.concatenate(qs, axis=0)

    kc = kc_ref[...]
    s_t = lax.dot_general(kc, q_all.astype(F32), (((1,), (1,)), ((), ())),
                          preferred_element_type=F32, precision=HIGHEST)
    blk4 = lax.broadcasted_iota(I32, s_t.shape, 0)
    qpos4 = qpos0 + lax.broadcasted_iota(I32, s_t.shape, 1) % Q_TILE
    vis = (blk4 * BLOCK + (BLOCK - 1)) <= qpos4
    s_m = jnp.where(vis, s_t, NEG)
    e = jnp.exp(s_m - jnp.max(s_m, axis=0, keepdims=True))
    p_t = e / jnp.sum(e, axis=0, keepdims=True)
    p_t = p_t * jnp.where(vis, 1.0, 0.0)
    o_c = jnp.dot(p_t.T.astype(BF16), vc_ref[...].astype(BF16), preferred_element_type=F32)

    imp = p_t[:, 0:Q_TILE]
    for r in range(1, GQ):
        imp = imp + p_t[:, r * Q_TILE:(r + 1) * Q_TILE]
    blk = lax.broadcasted_iota(I32, imp.shape, 0)
    qpos_l = qpos0 + lax.broadcasted_iota(I32, imp.shape, 1)
    sel = _select_blocks(imp, blk, qpos_l // BLOCK, nblk)
    sel_q = sel.T.astype(BF16)

    qpos = qpos0 + lax.broadcasted_iota(I32, (Q_TILE, 1), 0)
    m_scr[...] = jnp.full(m_scr.shape, NEG, F32)
    l_scr[...] = jnp.zeros(l_scr.shape, F32)
    acc_scr[...] = jnp.zeros(acc_scr.shape, F32)

    def key_tile(kt, _):
        k0 = pl.multiple_of(kt * KEY_TILE, KEY_TILE)
        picked = jnp.dot(sel_q, e_ref[kt], preferred_element_type=F32)
        kpos = k0 + lax.broadcasted_iota(I32, (Q_TILE, KEY_TILE), 1)
        valid = (picked > 0.5) & (kpos <= qpos)
        k_t = ks_ref[pl.ds(k0, KEY_TILE), :]
        v_t = vs_ref[pl.ds(k0, KEY_TILE), :]
        for r in range(GQ):
            s = lax.dot_general(qs[r], k_t, (((1,), (1,)), ((), ())), preferred_element_type=F32)
            s = jnp.where(valid, s, NEG)
            m_old = m_scr[r]
            m_new = jnp.maximum(m_old, jnp.max(s, axis=-1, keepdims=True))
            alpha = jnp.exp(m_old - m_new)
            p = jnp.exp(s - m_new)
            l_scr[r] = alpha * l_scr[r] + jnp.sum(p, axis=-1, keepdims=True)
            acc_scr[r] = alpha * acc_scr[r] + jnp.dot(p.astype(BF16), v_t, preferred_element_type=F32)
            m_scr[r] = m_new
        return 0
    lax.fori_loop(0, (qpos0 + Q_TILE + KEY_TILE - 1) // KEY_TILE, key_tile, 0)

    w0 = pl.multiple_of(jnp.maximum(qpos0 + Q_TILE - wkeys, 0), Q_TILE)
    k_w = kw_ref[pl.ds(w0, wkeys), :]
    v_w = vw_ref[pl.ds(w0, wkeys), :]
    wpos = w0 + lax.broadcasted_iota(I32, (Q_TILE, wkeys), 1)
    w_ok = (wpos <= qpos) & (wpos >= qpos - WINDOW)

    g = gt_ref[...]
    for r in range(GQ):
        s_w = lax.dot_general(qs[r], k_w, (((1,), (1,)), ((), ())), preferred_element_type=F32)
        p_w = _softmax_rows(jnp.where(w_ok, s_w, NEG))
        o_w = jnp.dot(p_w.astype(BF16), v_w, preferred_element_type=F32)
        o_s = acc_scr[r] / l_scr[r]
        o = (g[:, 3 * r:3 * r + 1] * o_c[r * Q_TILE:(r + 1) * Q_TILE]
             + g[:, 3 * r + 1:3 * r + 2] * o_s + g[:, 3 * r + 2:3 * r + 3] * o_w)
        o_ref[:, r * HEAD_DIM:(r + 1) * HEAD_DIM] = o


def _block_expand(n_keys):
    kt = n_keys // KEY_TILE
    key_blk = (jnp.arange(kt)[:, None, None] * KEY_TILE + jnp.arange(KEY_TILE)[None, None, :]) // BLOCK
    return (key_blk == jnp.arange(LANE)[None, :, None]).astype(BF16)


def _nsa_prompt(q, gates, cmp_kv, kvb, winb, batch, seq):
    nblk = seq // BLOCK
    assert nblk <= LANE and seq % KEY_TILE == 0
    nq = seq // Q_TILE
    wkeys = min(WINDOW + Q_TILE, seq)
    kc = jnp.pad(cmp_kv, ((0, 0), (0, 0), (0, 0), (0, LANE - nblk), (0, 0)))
    kvcol = lambda c0: pl.BlockSpec((seq, HEAD_DIM), lambda b, g, i, c0=c0: (b, c0 + g))
    return pl.pallas_call(
        functools.partial(_nsa_prompt_body, seq=seq, nblk=nblk, wkeys=wkeys),
        grid=(batch, N_KV, nq),
        in_specs=[pl.BlockSpec((Q_TILE, GQ * HEAD_DIM), lambda b, g, i: (b * nq + i, g)),
                  pl.BlockSpec((Q_TILE, LANE), lambda b, g, i: (b * nq + i, g)),
                  pl.BlockSpec((None, None, None, LANE, HEAD_DIM), lambda b, g, i: (b, 0, g, 0, 0)),
                  pl.BlockSpec((None, None, None, LANE, HEAD_DIM), lambda b, g, i: (b, 1, g, 0, 0)),
                  kvcol(2 * N_KV), kvcol(3 * N_KV), kvcol(0), kvcol(N_KV),
                  pl.BlockSpec((seq // KEY_TILE, LANE, KEY_TILE), lambda b, g, i: (0, 0, 0))],
        out_specs=pl.BlockSpec((Q_TILE, GQ * HEAD_DIM), lambda b, g, i: (b * nq + i, g)),
        out_shape=jax.ShapeDtypeStruct((batch * seq, N_KV * GQ * HEAD_DIM), F32),
        scratch_shapes=[pltpu.VMEM((GQ, Q_TILE, 1), F32), pltpu.VMEM((GQ, Q_TILE, 1), F32),
                        pltpu.VMEM((GQ, Q_TILE, HEAD_DIM), F32)],
        compiler_params=_cparams(("arbitrary", "arbitrary", "arbitrary"), 40),
        name="nsa_prompt",
    )(q, gates, kc, kc, kvb, kvb, winb, winb, _block_expand(seq))


def _nsa_sample_body(pt_ref, pg0_ref, pg1_ref, pg2_ref, pg3_ref, q_ref, gt_ref, kvn_ref, wn_ref, cw_ref,
                     pe_ref, w1_ref, w2_ref, gkc_ref, e_ref, o_ref,
                     cmp_scr, ks_scr, vs_scr, kw_scr, vw_scr, *, past, nt, wbuf, page, ppg):
    s = pl.program_id(0)
    pg = pl.program_id(1)
    nb_tot = -(-(past + nt) // BLOCK)
    nblk = cmp_scr.shape[1] // BLOCK
    kp = ks_scr.shape[0]
    wk = kw_scr.shape[0]

    @pl.when((s == 0) & (pg == 0))
    def _():
        cmp_scr[:, past:, :] = jnp.zeros((2 * N_KV, cmp_scr.shape[1] - past, HEAD_DIM), F32)
        ks_scr[past:, :] = jnp.zeros((kp - past, KV_W), BF16)
        vs_scr[past:, :] = jnp.zeros((kp - past, KV_W), BF16)
        kw_scr[wbuf:, :] = jnp.zeros((wk - wbuf, KV_W), BF16)
        vw_scr[wbuf:, :] = jnp.zeros((wk - wbuf, KV_W), BF16)

    for k, pref in enumerate((pg0_ref, pg1_ref, pg2_ref, pg3_ref)):
        r0 = pl.multiple_of((pg * ppg + k) * page, page)
        for h in range(2 * N_KV):
            cmp_scr[h, pl.ds(r0, page), :] = pref[:, h * HEAD_DIM:(h + 1) * HEAD_DIM]
        ks_scr[pl.ds(r0, page), :] = pref[:, 2 * KV_W:3 * KV_W].astype(BF16)
        vs_scr[pl.ds(r0, page), :] = pref[:, 3 * KV_W:4 * KV_W].astype(BF16)

    @pl.when(pg == pl.num_programs(1) - 1)
    def _():
        kvn = kvn_ref[...]
        for h in range(2 * N_KV):
            cmp_scr[h, past:past + nt, :] = kvn[:, h * HEAD_DIM:(h + 1) * HEAD_DIM]
        ks_scr[past:past + nt, :] = kvn[:, 2 * KV_W:3 * KV_W].astype(BF16)
        vs_scr[past:past + nt, :] = kvn[:, 3 * KV_W:4 * KV_W].astype(BF16)
        cw = cw_ref[...]
        wn = wn_ref[...]
        kw_scr[0:wbuf, :] = cw[:, 0:KV_W].astype(BF16)
        vw_scr[0:wbuf, :] = cw[:, KV_W:2 * KV_W].astype(BF16)
        kw_scr[wbuf:wbuf + nt, :] = wn[:, 0:KV_W].astype(BF16)
        vw_scr[wbuf:wbuf + nt, :] = wn[:, KV_W:2 * KV_W].astype(BF16)

        cmp_out = []
        for slot in range(2):
            heads = [cmp_scr.at[slot * N_KV + h] for h in range(N_KV)]
            hid = jax.nn.gelu(_compress_rows(pe_ref.at[slot], w1_ref.at[slot], nblk, heads))
            out = jnp.dot(hid.astype(BF16), w2_ref[slot], preferred_element_type=F32)
            cmp_out.append(_rms(out, gkc_ref[...]) if slot == 0 else out)
        zpad = jnp.zeros((LANE - nblk, HEAD_DIM), F32)

        rows = GQ * nt
        t_of_row = lax.broadcasted_iota(I32, (rows, 1), 0) % nt
        qpos = past + t_of_row
        qf = q_ref[...]
        gt = gt_ref[...]
        blk = lax.broadcasted_iota(I32, (rows, LANE), 1)
        vis = (blk * BLOCK + (BLOCK - 1)) <= qpos
        blk_t = lax.broadcasted_iota(I32, (nt, LANE), 1)
        qblk_t = (past + lax.broadcasted_iota(I32, (nt, 1), 0)) // BLOCK
        kpos = lax.broadcasted_iota(I32, (rows, kp), 1)
        wlane = lax.broadcasted_iota(I32, (rows, wk), 1)
        wpos = (past - wbuf) + wlane
        w_ok = (wpos <= qpos) & (wpos >= qpos - WINDOW) & (wlane < wbuf + nt)

        for g in range(N_KV):
            qg32 = jnp.concatenate(
                [qf[:, (g * GQ + r) * HEAD_DIM:(g * GQ + r + 1) * HEAD_DIM] for r in range(GQ)], axis=0)
            qg = qg32.astype(BF16)
            kc = jnp.concatenate([cmp_out[0][g * nblk:(g + 1) * nblk], zpad], axis=0)
            vc = jnp.concatenate([cmp_out[1][g * nblk:(g + 1) * nblk], zpad], axis=0)
            s_c = lax.dot_general(qg32, kc, (((1,), (1,)), ((), ())),
                                  preferred_element_type=F32, precision=HIGHEST)
            p_c = _softmax_rows(jnp.where(vis, s_c, NEG)) * jnp.where(vis, 1.0, 0.0)
            o_c = jnp.dot(p_c.astype(BF16), vc.astype(BF16), preferred_element_type=F32)
            imp = p_c[0:nt]
            for r in range(1, GQ):
                imp = imp + p_c[r * nt:(r + 1) * nt]
            sel = _select_blocks(imp, blk_t, qblk_t, nb_tot, axis=1)
            picked = jnp.dot(sel.astype(BF16), e_ref[...], preferred_element_type=F32)
            picked = jnp.concatenate([picked] * GQ, axis=0)
            valid = (picked > 0.5) & (kpos <= qpos)
            cols = slice(g * HEAD_DIM, (g + 1) * HEAD_DIM)
            s_s = lax.dot_general(qg, ks_scr[:, cols], (((1,), (1,)), ((), ())), preferred_element_type=F32)
            p_s = _softmax_rows(jnp.where(valid, s_s, NEG))
            o_s = jnp.dot(p_s.astype(BF16), vs_scr[:, cols], preferred_element_type=F32)
            s_w = lax.dot_general(qg, kw_scr[:, cols], (((1,), (1,)), ((), ())), preferred_element_type=F32)
            p_w = _softmax_rows(jnp.where(w_ok, s_w, NEG))
            o_w = jnp.dot(p_w.astype(BF16), vw_scr[:, cols], preferred_element_type=F32)
            gcol = lambda br: jnp.concatenate(
                [gt[:, g * LANE + 3 * r + br:g * LANE + 3 * r + br + 1] for r in range(GQ)], axis=0)
            o = gcol(0) * o_c + gcol(1) * o_s + gcol(2) * o_w
            for r in range(GQ):
                h = g * GQ + r
                o_ref[:, h * HEAD_DIM:(h + 1) * HEAD_DIM] = o[r * nt:(r + 1) * nt]


def _nsa_sample(q_bt, gates_bt, kvnew_bt, winnew_bt, cache_pages, cache_win2, page_table, cw):
    nb, nt, _ = q_bt.shape
    n_pages = page_table.shape[1]
    page = cache_pages.shape[1]
    past = n_pages * page
    wbuf = cache_win2.shape[1]
    ppg = 4
    assert n_pages % ppg == 0 and past % BLOCK == 0 and past >= wbuf and nt <= 8
    nb_tot = -(-(past + nt) // BLOCK)
    nblk = -(-nb_tot // 8) * 8
    kp = -(-(past + nt) // LANE) * LANE
    wk = -(-(wbuf + nt) // LANE) * LANE
    assert nblk <= LANE
    pe, w1, w2, g_kc = cw
    key_blk = jnp.arange(kp)[None, :] // BLOCK
    expand = (key_blk == jnp.arange(LANE)[:, None]).astype(BF16)
    seq3 = lambda n: pl.BlockSpec((None, nt, n), lambda s, g, pt: (s, 0, 0))
    const = lambda shape: pl.BlockSpec(shape, lambda s, g, pt: (0,) * len(shape))
    page_spec = lambda k: pl.BlockSpec(
        (None, page, 4 * KV_W), lambda s, g, pt, k=k: (pt[s * n_pages + g * ppg + k], 0, 0))
    grid_spec = pltpu.PrefetchScalarGridSpec(
        num_scalar_prefetch=1,
        grid=(nb, n_pages // ppg),
        in_specs=[page_spec(0), page_spec(1), page_spec(2), page_spec(3),
                  seq3(N_KV * GQ * HEAD_DIM), seq3(N_KV * LANE), seq3(4 * KV_W), seq3(2 * KV_W),
                  pl.BlockSpec((None, wbuf, 2 * KV_W), lambda s, g, pt: (s, 0, 0)),
                  const(pe.shape), const(w1.shape), const(w2.shape), const(g_kc.shape), const(expand.shape)],
        out_specs=pl.BlockSpec((None, nt, N_KV * GQ * HEAD_DIM), lambda s, g, pt: (s, 0, 0)),
        scratch_shapes=[pltpu.VMEM((2 * N_KV, nblk * BLOCK, HEAD_DIM), F32),
                        pltpu.VMEM((kp, KV_W), BF16), pltpu.VMEM((kp, KV_W), BF16),
                        pltpu.VMEM((wk, KV_W), BF16), pltpu.VMEM((wk, KV_W), BF16)])
    return pl.pallas_call(
        functools.partial(_nsa_sample_body, past=past, nt=nt, wbuf=wbuf, page=page, ppg=ppg),
        grid_spec=grid_spec,
        out_shape=jax.ShapeDtypeStruct((nb, nt, N_KV * GQ * HEAD_DIM), F32),
        compiler_params=_cparams(("arbitrary", "arbitrary"), 56),
        name="nsa_sample",
    )(page_table.reshape(-1), cache_pages, cache_pages, cache_pages, cache_pages,
      q_bt, gates_bt, kvnew_bt, winnew_bt, cache_win2, pe, w1, w2, g_kc, expand)


def _merge_body(sn_ref, at_ref, ga_ref, x_ref, w_ref, g_ref, o_ref, cat_scr):
    @pl.when(pl.program_id(1) == 0)
    def _():
        d2 = sn_ref.shape[1]
        cat_scr[:, 0:d2] = sn_ref[...]
        cat_scr[:, d2:] = _rms(at_ref[...], g_ref[...]).astype(BF16)

    acc = jnp.dot(cat_scr[...], w_ref[...], preferred_element_type=F32)
    o_ref[...] = x_ref[...] + ga_ref[...] * acc


def _merge(sn, attn, ga, ga_spec, x2d, x_spec, w_out_b, g_attn, n_rows, tm):
    d2 = sn.shape[1]
    d = w_out_b.shape[1]
    tn = 512
    return pl.pallas_call(
        _merge_body,
        grid=(n_rows // tm, d // tn),
        in_specs=[pl.BlockSpec((tm, d2), lambda i, j: (i, 0)),
                  pl.BlockSpec((tm, d2), lambda i, j: (i, 0)),
                  ga_spec, x_spec,
                  pl.BlockSpec((2 * d2, tn), lambda i, j: (0, j)),
                  pl.BlockSpec((1, d2), lambda i, j: (0, 0))],
        out_specs=pl.BlockSpec((tm, tn), lambda i, j: (i, j)),
        out_shape=jax.ShapeDtypeStruct((n_rows, d), F32),
        scratch_shapes=[pltpu.VMEM((tm, 2 * d2), BF16)],
        compiler_params=_cparams(("arbitrary", "arbitrary"), 48),
        name="merge",
    )(sn, attn, ga, x2d, w_out_b, g_attn)


SLAB = 40
SLAB_USED = 32


def _route_core(x, sh, sc, g2_ref, wr_ref, br_ref, h3_ref, rt_ref):
    tm, d = x.shape
    h = _rms(x, g2_ref[...]) * (1.0 + sc) + sh
    for j in range(d // LANE):
        h3_ref[pl.ds(j, tm, stride=SLAB), :] = h[:, j * LANE:(j + 1) * LANE]
    zero = jnp.zeros((tm, LANE), F32)
    for j in range(SLAB_USED, SLAB):
        h3_ref[pl.ds(j, tm, stride=SLAB), :] = zero
    logits = jnp.dot(h, wr_ref[...], preferred_element_type=F32, precision=HIGHEST) + br_ref[...]
    lane = lax.broadcasted_iota(I32, logits.shape, 1)
    big = jnp.int32(1 << 20)
    first = lambda mask: jnp.min(jnp.where(mask, lane, big), axis=-1, keepdims=True)
    is_g = lane < N_GROUPS
    gmax = jnp.max(jnp.where(is_g, logits, NEG), axis=-1, keepdims=True)
    g_sel = first(is_g & (logits == gmax))
    p_grp = 1.0 / jnp.sum(jnp.where(is_g, jnp.exp(logits - gmax), 0.0), axis=-1, keepdims=True)
    lo = N_GROUPS + EXPERTS_PER_GROUP * g_sel
    in_grp = (lane >= lo) & (lane < lo + EXPERTS_PER_GROUP)
    emax = jnp.max(jnp.where(in_grp, logits, NEG), axis=-1, keepdims=True)
    pe = jnp.where(in_grp, jnp.exp(logits - emax), 0.0)
    prob = pe / jnp.sum(pe, axis=-1, keepdims=True)
    p1 = jnp.max(jnp.where(in_grp, prob, -1.0), axis=-1, keepdims=True)
    i1 = first(in_grp & (prob == p1))
    rest = in_grp & (lane != i1)
    p2 = jnp.max(jnp.where(rest, prob, -1.0), axis=-1, keepdims=True)
    i2 = first(rest & (prob == p2))
    w1 = p_grp * p1 / (p1 + p2)
    w2 = p_grp * p2 / (p1 + p2)
    e1 = (i1 - N_GROUPS).astype(F32)
    e2 = (i2 - N_GROUPS).astype(F32)
    rt_ref[...] = jnp.where(lane == 0, e1, jnp.where(lane == 1, e2, jnp.where(lane == 2, w1, jnp.where(lane == 3, w2, 0.0))))


def _route_body(xp_ref, xs_ref, shp_ref, scp_ref, shs_ref, scs_ref, g2_ref, wr_ref, br_ref,
                h3_ref, rt_ref, *, ntp):
    i = pl.program_id(0)

    @pl.when(i < ntp)
    def _():
        _route_core(xp_ref[...], shp_ref[...], scp_ref[...], g2_ref, wr_ref, br_ref, h3_ref, rt_ref)

    @pl.when(i >= ntp)
    def _():
        _route_core(xs_ref[...], shs_ref[...], scs_ref[...], g2_ref, wr_ref, br_ref, h3_ref, rt_ref)


def _route(x1_p, x1_s, sh_p, sc_p, sh_s, sc_s, g2, w_r, b_r, seq, tm):
    n_p, d = x1_p.shape
    n_s = x1_s.shape[0]
    assert n_p % tm == 0 and n_s % tm == 0 and seq % tm == 0
    ntp, nts = n_p // tm, n_s // tm
    n_all = n_p + n_s
    tpb = seq // tm
    pidx = lambda i: jnp.minimum(i, ntp - 1)
    sidx = lambda i: jnp.maximum(i - ntp, 0)
    pmod = pl.BlockSpec((None, 1, d), lambda i: (pidx(i) // tpb, 0, 0))
    smod = pl.BlockSpec((tm, d), lambda i: (sidx(i), 0))
    return pl.pallas_call(
        functools.partial(_route_body, ntp=ntp),
        grid=(ntp + nts,),
        in_specs=[pl.BlockSpec((tm, d), lambda i: (pidx(i), 0)),
                  pl.BlockSpec((tm, d), lambda i: (sidx(i), 0)),
                  pmod, pmod, smod, smod,
                  pl.BlockSpec((1, d), lambda i: (0, 0)),
                  pl.BlockSpec((d, LANE), lambda i: (0, 0)),
                  pl.BlockSpec((1, LANE), lambda i: (0, 0))],
        out_specs=[pl.BlockSpec((tm * SLAB, LANE), lambda i: (i, 0)),
                   pl.BlockSpec((tm, LANE), lambda i: (i, 0))],
        out_shape=[jax.ShapeDtypeStruct((n_all * SLAB, LANE), F32),
                   jax.ShapeDtypeStruct((n_all, LANE), F32)],
        compiler_params=_cparams(("arbitrary",), 48),
        name="route",
    )(x1_p, x1_s, sh_p, sc_p, sh_s, sc_s, g2, w_r, b_r)


def _slab_dma(src_ref, src_row, dst_ref, dst_row, sem):
    return pltpu.make_async_copy(
        src_ref.at[pl.ds(pl.multiple_of(src_row * SLAB, 8), SLAB_USED), :],
        dst_ref.at[pl.ds(pl.multiple_of(dst_row * SLAB, 8), SLAB_USED), :], sem)


def _moe_body(te_ref, na_ref, tok_ref, h3_hbm, ws_ref, w1_ref, w3_ref, w2_ref, y_ref,
              xbuf, sem, a1_scr, a3_scr, hid_scr):
    t = pl.program_id(0)
    s = pl.program_id(1)
    tm = MOE_TM
    n_act = na_ref[0]
    slot = t % 2
    kc = w1_ref.shape[0]
    nk = kc // LANE

    def gather(tile, buf, start):
        def body(r, _):
            cp = _slab_dma(h3_hbm, tok_ref[tile * tm + r], xbuf.at[buf], r, sem.at[buf])
            if start:
                cp.start()
            else:
                cp.wait()
            return 0
        lax.fori_loop(0, tm, body, 0)

    @pl.when((s == 0) & (t >= n_act))
    def _():
        y_ref[...] = jnp.zeros(y_ref.shape, F32)

    @pl.when((s == 0) & (t == 0) & (n_act > 0))
    def _():
        gather(0, 0, True)

    @pl.when((s == 0) & (t < n_act))
    def _():
        gather(t, slot, False)

    @pl.when((s == 0) & (t + 1 < n_act))
    def _():
        gather(t + 1, 1 - slot, True)

    @pl.when((t < n_act) & (s < 4))
    def _():
        xb = xbuf.at[slot]
        acc1 = jnp.zeros((tm, w1_ref.shape[1]), F32)
        acc3 = jnp.zeros((tm, w3_ref.shape[1]), F32)
        for jj in range(0, nk, 2):
            x = jnp.concatenate(
                [xb[pl.ds(s * nk + jj + u, tm, stride=SLAB), :] for u in range(2)], axis=-1).astype(BF16)
            rows = slice(jj * LANE, (jj + 2) * LANE)
            acc1 = acc1 + jnp.dot(x, w1_ref[rows, :].astype(BF16), preferred_element_type=F32)
            acc3 = acc3 + jnp.dot(x, w3_ref[rows, :].astype(BF16), preferred_element_type=F32)

        @pl.when(s == 0)
        def _():
            a1_scr[...] = acc1
            a3_scr[...] = acc3

        @pl.when(s > 0)
        def _():
            a1_scr[...] += acc1
            a3_scr[...] += acc3

        @pl.when(s == 3)
        def _():
            a1 = a1_scr[...]
            hid = a1 * jax.nn.sigmoid(a1) * a3_scr[...]
            hid_scr[...] = (hid * ws_ref[...]).astype(BF16)

    @pl.when((t < n_act) & (s >= 4))
    def _():
        res = jnp.dot(hid_scr[...], w2_ref[...].astype(BF16), preferred_element_type=F32)
        nn = res.shape[1] // LANE
        for jj in range(nn):
            y_ref[pl.ds((s - 4) * nn + jj, tm, stride=SLAB), :] = res[:, jj * LANE:(jj + 1) * LANE]

        @pl.when(s == 4)
        def _():
            zero = jnp.zeros((tm, LANE), F32)
            for j in range(SLAB_USED, SLAB):
                y_ref[pl.ds(j, tm, stride=SLAB), :] = zero


def _moe(h3, tile_e, n_act, slot_tok, slot_w, w_e1, w_e3, w_e2, n_tiles):
    tm = MOE_TM
    _, d, f = w_e1.shape
    assert d // LANE == SLAB_USED and d % 4 == 0
    kc = d // 4
    act = lambda t, na: t < na[0]
    last = lambda t, na: jnp.minimum(t, jnp.maximum(na[0] - 1, 0))
    grid_spec = pltpu.PrefetchScalarGridSpec(
        num_scalar_prefetch=3,
        grid=(n_tiles, 8),
        in_specs=[pl.BlockSpec(memory_space=pl.ANY),
                  pl.BlockSpec((tm, 1), lambda t, s, te, na, tok: (last(t, na), 0)),
                  pl.BlockSpec((None, kc, f), lambda t, s, te, na, tok:
                               (te[t], jnp.where(act(t, na), jnp.minimum(s, 3), 3), 0)),
                  pl.BlockSpec((None, kc, f), lambda t, s, te, na, tok:
                               (te[t], jnp.where(act(t, na), jnp.minimum(s, 3), 3), 0)),
                  pl.BlockSpec((None, f, kc), lambda t, s, te, na, tok:
                               (te[t], 0, jnp.where(act(t, na), jnp.maximum(s - 4, 0), 3)))],
        out_specs=pl.BlockSpec((tm * SLAB, LANE), lambda t, s, te, na, tok: (t, 0)),
        scratch_shapes=[pltpu.VMEM((2, tm * SLAB, LANE), F32),
                        pltpu.SemaphoreType.DMA((2,)),
                        pltpu.VMEM((tm, f), F32), pltpu.VMEM((tm, f), F32), pltpu.VMEM((tm, f), BF16)])
    return pl.pallas_call(
        _moe_body,
        grid_spec=grid_spec,
        out_shape=jax.ShapeDtypeStruct((n_tiles * tm * SLAB, LANE), F32),
        compiler_params=_cparams(("arbitrary", "arbitrary"), 56),
        name="moe_ffn",
    )(tile_e, n_act, slot_tok, h3, slot_w, w_e1, w_e3, w_e2)


def _dispatch(rt, n_tiles):
    tm = MOE_TM
    n = rt.shape[0]
    e_flat = rt[:, 0:2].astype(I32).T.reshape(-1)
    w_flat = rt[:, 2:4].T.reshape(-1)
    onehot = (e_flat[:, None] == jnp.arange(N_EXPERTS, dtype=I32)[None, :]).astype(I32)
    csum = jnp.cumsum(onehot, axis=0)
    pos_in = jnp.sum(onehot * (csum - onehot), axis=1)
    counts = csum[-1]
    padded = (counts + tm - 1) // tm * tm
    start = jnp.cumsum(padded) - padded
    slot = start[e_flat] + pos_in
    n_slots = n_tiles * tm
    slot_tok = jnp.zeros((n_slots,), I32).at[slot].set(jnp.arange(2 * n, dtype=I32) % n)
    slot_w = jnp.zeros((n_slots,), F32).at[slot].set(w_flat)
    n_act = (jnp.sum(padded) // tm).astype(I32)
    tiles = jnp.arange(n_tiles, dtype=I32)
    tile_e = jnp.sum((start // tm)[None, :] <= tiles[:, None], axis=1).astype(I32) - 1
    tile_e = tile_e[jnp.minimum(tiles, jnp.maximum(n_act - 1, 0))]
    return tile_e, n_act.reshape(1), slot_tok, slot_w.reshape(n_slots, 1), slot.reshape(2, n)


def _combine_body(p0_ref, p1_ref, y_hbm, xp_ref, xs_ref, gp_ref, gs_ref, op_ref, os_ref,
                  ybuf, sem, *, ntp, tm):
    i = pl.program_id(0)
    n_i = pl.num_programs(0)
    slot = i % 2

    def gather(tile, buf, start):
        def body(r, _):
            for k, pref in enumerate((p0_ref, p1_ref)):
                cp = _slab_dma(y_hbm, pref[tile * tm + r], ybuf.at[buf, k], r, sem.at[buf])
                if start:
                    cp.start()
                else:
                    cp.wait()
            return 0
        lax.fori_loop(0, tm, body, 0)

    @pl.when(i == 0)
    def _():
        gather(0, 0, True)

    gather(i, slot, False)

    @pl.when(i + 1 < n_i)
    def _():
        gather(i + 1, 1 - slot, True)

    def emit(x_ref, g_ref, o_ref):
        for j in range(SLAB_USED):
            cols = slice(j * LANE, (j + 1) * LANE)
            y = (ybuf[slot, 0, pl.ds(j, tm, stride=SLAB), :] + ybuf[slot, 1, pl.ds(j, tm, stride=SLAB), :])
            o_ref[:, cols] = x_ref[:, cols] + g_ref[:, cols] * y

    @pl.when(i < ntp)
    def _():
        emit(xp_ref, gp_ref, op_ref)

    @pl.when(i >= ntp)
    def _():
        emit(xs_ref, gs_ref, os_ref)


def _combine(y_slabs, pos, x1_p, x1_s, ga_p, ga_s, seq, tm):
    n_p, d = x1_p.shape
    n_s = x1_s.shape[0]
    ntp, nts = n_p // tm, n_s // tm
    tpb = seq // tm
    pidx = lambda i: jnp.minimum(i, ntp - 1)
    sidx = lambda i: jnp.maximum(i - ntp, 0)
    grid_spec = pltpu.PrefetchScalarGridSpec(
        num_scalar_prefetch=2,
        grid=(ntp + nts,),
        in_specs=[pl.BlockSpec(memory_space=pl.ANY),
                  pl.BlockSpec((tm, d), lambda i, p0, p1: (pidx(i), 0)),
                  pl.BlockSpec((tm, d), lambda i, p0, p1: (sidx(i), 0)),
                  pl.BlockSpec((None, 1, d), lambda i, p0, p1: (pidx(i) // tpb, 0, 0)),
                  pl.BlockSpec((tm, d), lambda i, p0, p1: (sidx(i), 0))],
        out_specs=[pl.BlockSpec((tm, d), lambda i, p0, p1: (pidx(i), 0)),
                   pl.BlockSpec((tm, d), lambda i, p0, p1: (sidx(i), 0))],
        scratch_shapes=[pltpu.VMEM((2, 2, tm * SLAB, LANE), F32), pltpu.SemaphoreType.DMA((2,))])
    return pl.pallas_call(
        functools.partial(_combine_body, ntp=ntp, tm=tm),
        grid_spec=grid_spec,
        out_shape=[jax.ShapeDtypeStruct((n_p, d), F32), jax.ShapeDtypeStruct((n_s, d), F32)],
        compiler_params=_cparams(("arbitrary",), 56),
        name="combine",
    )(pos[0], pos[1], y_slabs, x1_p, x1_s, ga_p, ga_s)


def _prep_w_in(w_in):
    d = w_in.shape[0]
    n_main = w_in.shape[1] - 3 * N_KV * GQ
    gates = w_in[:, n_main:].reshape(d, N_KV, GQ * 3)
    gates = jnp.pad(gates, ((0, 0), (0, 0), (0, LANE - GQ * 3))).reshape(d, N_KV * LANE)
    return jnp.concatenate([w_in[:, :n_main], gates], axis=1).astype(BF16)


def _inproj_prompt(x_p, sh, sc, p, w_in_p):
    b, seq, d = x_p.shape
    tm = 512
    tpb = seq // tm
    mod = pl.BlockSpec((None, 1, d), lambda i, j: (i // tpb, 0, 0))
    return _inproj(x_p.reshape(b * seq, d), pl.BlockSpec((tm, d), lambda i, j: (i, 0)),
                   sh.reshape(b, 1, d), sc.reshape(b, 1, d), mod, b * seq, tm,
                   p['g_norm1'].reshape(1, d), w_in_p, p['g_q'].reshape(1, -1), p['g_ks'].reshape(1, -1),
                   p['g_kw'].reshape(1, -1), d // 2)


def _inproj_sample(x_s, sh, sc, p, w_in_p):
    nb, nt, d = x_s.shape
    mod = pl.BlockSpec((nb, d), lambda i, j: (0, 0))
    return _inproj(x_s.reshape(nb, nt * d), pl.BlockSpec((nb, d), lambda i, j: (0, i)),
                   sh, sc, mod, nb * nt, nb,
                   p['g_norm1'].reshape(1, d), w_in_p, p['g_q'].reshape(1, -1), p['g_ks'].reshape(1, -1),
                   p['g_kw'].reshape(1, -1), d // 2)


def _compress_prep(p):
    hidden = p['w_ck1'].shape[-1]
    pe = jnp.stack([p['pe_k'], p['pe_v']])
    w1 = jnp.stack([p['w_ck1'], p['w_cv1']]).reshape(2, BLOCK * HEAD_DIM, hidden).astype(BF16)
    w2 = jnp.stack([p['w_ck2'], p['w_cv2']]).astype(BF16)
    return pe, w1, w2, p['g_kc'].reshape(1, HEAD_DIM)


def _nsa_run_prompt(op, cw, batch, seq):
    _, q, _, _, gates, kvb, winb, cmp_hm = op
    cmp_kv = _compress_prompt(cmp_hm, *cw, batch, seq)
    return _nsa_prompt(q, gates, cmp_kv, kvb, winb, batch, seq)


def _nsa_run_sample(os_, cache_kv, cache_win, page_table, cw, nb, nt):
    _, q, kv, win, gates, _, _, _ = os_
    to_bt = lambda a: a.reshape(nt, nb, a.shape[-1]).transpose(1, 0, 2)
    pages = cache_kv.reshape(cache_kv.shape[0], cache_kv.shape[1], -1)
    cwin = cache_win.reshape(nb, cache_win.shape[1], -1)
    attn = _nsa_sample(to_bt(q.astype(F32)), to_bt(gates), to_bt(kv), to_bt(win), pages, cwin, page_table, cw)
    return attn.transpose(1, 0, 2).reshape(nt * nb, -1)


def _s5_prep(p):
    ab_re, ab_im, bb_re, bb_im = _discretize(
        p['ssm_a_re'], p['ssm_a_im'], p['ssm_log_dt'],
        p['ssm_b_re'].transpose(2, 0, 1), p['ssm_b_im'].transpose(2, 0, 1))
    return _s5_weights(ab_re, ab_im, bb_re, bb_im, p['ssm_c_re'], p['ssm_c_im'])


def _s5_run_prompt(u, s5w, p, batch, seq):
    bd, cd, a_re, a_im = s5w
    d_ssm = u.shape[1]
    zeros = jnp.zeros((batch, 1, bd.shape[0] * S5_CHUNK_ST), F32)
    z, s_re, s_im = _s5_prompt(u, bd.astype(BF16), cd, a_re, a_im, p['ssm_d'].reshape(1, d_ssm),
                               zeros, zeros, batch, seq)
    sn = _glu(z, p['w_glu'].astype(BF16), p['b_glu'].reshape(1, d_ssm), p['g_out_ssm'].reshape(1, d_ssm), 512)
    return sn, s_re, s_im


def _s5_run_sample(u, s5w, p, st_re, st_im, nb, nt):
    bd, cd, a_re, a_im = s5w
    d_ssm = u.shape[1]
    z, s_re, s_im = _s5_sample(u, bd, cd, a_re, a_im, p['ssm_d'].reshape(1, d_ssm),
                               st_re.reshape(nb, -1), st_im.reshape(nb, -1), nb, nt)
    sn = _glu(z, p['w_glu'].astype(BF16), p['b_glu'].reshape(1, d_ssm), p['g_out_ssm'].reshape(1, d_ssm),
              min(512, nb * nt))
    return sn, s_re, s_im


def kernel(x_prompt, x_sample, c_prompt, c_sample, cache_kv, cache_win, state_ssm_re, state_ssm_im, page_table, w_ada, b_ada, g_norm1, g_norm2, w_in, g_q, g_kc, g_ks, g_kw, pe_k, pe_v, w_ck1, w_ck2, w_cv1, w_cv2, ssm_a_re, ssm_a_im, ssm_log_dt, ssm_b_re, ssm_b_im, ssm_c_re, ssm_c_im, ssm_d, w_glu, b_glu, g_out_ssm, g_out_attn, w_out, w_rg, b_rg, w_rexp, b_rexp, w_e1, w_e3, w_e2):
    names = ('w_ada', 'b_ada', 'g_norm1', 'g_norm2', 'w_in', 'g_q', 'g_kc', 'g_ks', 'g_kw', 'pe_k', 'pe_v',
             'w_ck1', 'w_ck2', 'w_cv1', 'w_cv2', 'ssm_a_re', 'ssm_a_im', 'ssm_log_dt', 'ssm_b_re', 'ssm_b_im',
             'ssm_c_re', 'ssm_c_im', 'ssm_d', 'w_glu', 'b_glu', 'g_out_ssm', 'g_out_attn', 'w_out',
             'w_rg', 'b_rg', 'w_rexp', 'b_rexp', 'w_e1', 'w_e3', 'w_e2')
    vals = (w_ada, b_ada, g_norm1, g_norm2, w_in, g_q, g_kc, g_ks, g_kw, pe_k, pe_v,
            w_ck1, w_ck2, w_cv1, w_cv2, ssm_a_re, ssm_a_im, ssm_log_dt, ssm_b_re, ssm_b_im,
            ssm_c_re, ssm_c_im, ssm_d, w_glu, b_glu, g_out_ssm, g_out_attn, w_out,
            w_rg, b_rg, w_rexp, b_rexp, w_e1, w_e3, w_e2)
    p = {k: v[0] for k, v in zip(names, vals)}
    b, seq, d = x_prompt.shape
    nb, nt, _ = x_sample.shape
    nc = b + nb
    c_all = jnp.concatenate([c_prompt, c_sample, jnp.zeros((-nc % 8, d), F32)], axis=0)
    m = _ada(c_all, p['w_ada'], p['b_ada'])
    mp, ms = m[:b], m[b:nc]
    w_in_p = _prep_w_in(p['w_in'])
    op = _inproj_prompt(x_prompt, mp[:, :d], mp[:, d:2 * d], p, w_in_p)
    os_ = _inproj_sample(x_sample, ms[:, :d], ms[:, d:2 * d], p, w_in_p)
    s5w = _s5_prep(p)
    sn_p, re_p, im_p = _s5_run_prompt(op[0], s5w, p, b, seq)
    sn_s, re_s, im_s = _s5_run_sample(os_[0], s5w, p, state_ssm_re[0], state_ssm_im[0], nb, nt)
    cw = _compress_prep(p)
    at_p = _nsa_run_prompt(op, cw, b, seq)
    at_s = _nsa_run_sample(os_, cache_kv[0], cache_win[0], page_table, cw, nb, nt)
    y_p, y_s = _tail(x_prompt, x_sample, mp, ms, sn_p, at_p, sn_s, at_s, p)

    to_bt = lambda a: a.reshape(nt, nb, a.shape[-1]).transpose(1, 0, 2)
    wbuf = cache_win.shape[2]
    win_p = op[3].reshape(b, seq, 2, N_KV, HEAD_DIM)[:, seq - min(WINDOW, seq):]
    win_new = to_bt(os_[3]).reshape(nb, nt, 2, N_KV, HEAD_DIM)
    win_s = jnp.concatenate([cache_win[0], win_new], axis=1)[:, nt:nt + wbuf]
    g_p = p['ssm_a_re'].shape
    return (y_p.reshape(b, seq, d),
            to_bt(y_s),
            op[2].reshape(1, b, seq, 4, N_KV, HEAD_DIM),
            win_p[None],
            re_p.reshape(1, b, *g_p), im_p.reshape(1, b, *g_p),
            to_bt(os_[2]).reshape(1, nb, nt, 4, N_KV, HEAD_DIM),
            win_s[None],
            re_s.reshape(1, nb, *g_p), im_s.reshape(1, nb, *g_p))


def _tail(x_prompt, x_sample, mp, ms, sn_p, at_p, sn_s, at_s, p):
    b, seq, d = x_prompt.shape
    nb, nt, _ = x_sample.shape
    chunk = lambda a, k: a[:, k * d:(k + 1) * d]
    w_out_b = p['w_out'].astype(BF16)
    g_attn = p['g_out_attn'].reshape(1, -1)
    tn = 512
    tm_p = 512
    tpb = seq // tm_p
    x1_p = _merge(sn_p, at_p, chunk(mp, 2).reshape(b, 1, d),
                  pl.BlockSpec((None, 1, tn), lambda i, j: (i // tpb, 0, j)),
                  x_prompt.reshape(b * seq, d), pl.BlockSpec((tm_p, tn), lambda i, j: (i, j)),
                  w_out_b, g_attn, b * seq, tm_p)
    ncol = d // tn
    x1_s = _merge(sn_s, at_s, chunk(ms, 2), pl.BlockSpec((nb, tn), lambda i, j: (0, j)),
                  x_sample.reshape(nb, nt * d), pl.BlockSpec((nb, tn), lambda i, j: (0, i * ncol + j)),
                  w_out_b, g_attn, nt * nb, nb)

    tm = 128
    rep = lambda a: jnp.tile(a, (nt, 1))
    w_r = jnp.concatenate([p['w_rg'], p['w_rexp']], axis=1)
    b_r = jnp.concatenate([p['b_rg'], p['b_rexp']], axis=0)
    w_r = jnp.pad(w_r, ((0, 0), (0, LANE - w_r.shape[1])))
    b_r = jnp.pad(b_r, (0, LANE - b_r.shape[0])).reshape(1, LANE)
    h3, rt = _route(x1_p, x1_s, chunk(mp, 3).reshape(b, 1, d), chunk(mp, 4).reshape(b, 1, d),
                    rep(chunk(ms, 3)), rep(chunk(ms, 4)), p['g_norm2'].reshape(1, d), w_r, b_r, seq, tm)
    n_all = b * seq + nb * nt
    n_tiles = -(-2 * n_all // MOE_TM) + N_EXPERTS
    tile_e, n_act, slot_tok, slot_w, pos = _dispatch(rt, n_tiles)
    y_slabs = _moe(h3, tile_e, n_act, slot_tok, slot_w, p['w_e1'], p['w_e3'], p['w_e2'], n_tiles)
    return _combine(y_slabs, pos, x1_p, x1_s, chunk(mp, 5).reshape(b, 1, d), rep(chunk(ms, 5)), seq, tm)
```

```python
import functools
import math

import jax
import jax.numpy as jnp
from jax import lax
from jax.experimental import pallas as pl
from jax.experimental.pallas import tpu as pltpu

F32 = jnp.float32
BF16 = jnp.bfloat16
I32 = jnp.int32
HIGHEST = lax.Precision.HIGHEST

EPS = 1e-6
NEG = -1e30
FORCE_BONUS = 1e4

LANE = 128
HEAD_DIM = 128
N_KV = 4
GQ = 4
KV_W = N_KV * HEAD_DIM
BLOCK = 64
N_SEL = 16
WINDOW = 512
Q_TILE = 128
KEY_TILE = 512
SSM_CH = 16
SSM_STATE = 64
S5_CHUNK_CH = 128
S5_CHUNK_ST = S5_CHUNK_CH // SSM_CH * SSM_STATE
N_GROUPS = 4
EXPERTS_PER_GROUP = 8
N_EXPERTS = N_GROUPS * EXPERTS_PER_GROUP
PROJ_TN = 512
MOE_TM = 256


def _cparams(sem, vmem_mb):
    return pltpu.CompilerParams(dimension_semantics=sem, vmem_limit_bytes=vmem_mb << 20)


def _rms(x, g):
    return x * lax.rsqrt(jnp.mean(x * x, axis=-1, keepdims=True) + EPS) * g


def _ada_body(c_ref, w_ref, b_ref, o_ref):
    c = c_ref[...]
    a = (c * jax.nn.sigmoid(c)).astype(BF16)
    o_ref[...] = jnp.dot(a, w_ref[...].astype(BF16), preferred_element_type=F32) + b_ref[...]


def _ada(c_all, w_ada, b_ada):
    r, d = c_all.shape
    n = w_ada.shape[1]
    tn = 512
    return pl.pallas_call(
        _ada_body,
        grid=(n // tn,),
        in_specs=[pl.BlockSpec((r, d), lambda j: (0, 0)),
                  pl.BlockSpec((d, tn), lambda j: (0, j)),
                  pl.BlockSpec((1, tn), lambda j: (0, j))],
        out_specs=pl.BlockSpec((r, tn), lambda j: (0, j)),
        out_shape=jax.ShapeDtypeStruct((r, n), F32),
        compiler_params=_cparams(("arbitrary",), 40),
        name="ada",
    )(c_all, w_ada, b_ada.reshape(1, n))


def _head_norm(acc, g):
    return jnp.concatenate(
        [_rms(acc[:, h * HEAD_DIM:(h + 1) * HEAD_DIM], g) for h in range(N_KV)], axis=-1)


def _inproj_body(x_ref, sh_ref, sc_ref, g1_ref, w_ref, gq_ref, gks_ref, gkw_ref,
                 u_ref, q_ref, kv_ref, win_ref, gt_ref, kvb_ref, winb_ref, cmp_ref, h_scr, *, nu):
    j = pl.program_id(1)

    @pl.when(j == 0)
    def _():
        h = _rms(x_ref[...], g1_ref[...]) * (1.0 + sc_ref[...]) + sh_ref[...]
        h_scr[...] = h.astype(BF16)

    acc = jnp.dot(h_scr[...], w_ref[...], preferred_element_type=F32)

    @pl.when(j < nu)
    def _():
        u_ref[...] = acc

    @pl.when((j >= nu) & (j < 2 * nu))
    def _():
        q_ref[...] = (_head_norm(acc, gq_ref[...]) * (HEAD_DIM ** -0.5)).astype(BF16)

    jk = j - 2 * nu

    @pl.when((jk >= 0) & (jk < 4) & (jk != 2))
    def _():
        kv_ref[...] = acc
        kvb_ref[...] = acc.astype(BF16)

    @pl.when((jk == 0) | (jk == 1))
    def _():
        for h in range(N_KV):
            cmp_ref[h] = acc[:, h * HEAD_DIM:(h + 1) * HEAD_DIM]

    @pl.when(jk == 2)
    def _():
        kn = _head_norm(acc, gks_ref[...])
        kv_ref[...] = kn
        kvb_ref[...] = kn.astype(BF16)

    @pl.when(jk == 4)
    def _():
        kn = _head_norm(acc, gkw_ref[...])
        win_ref[...] = kn
        winb_ref[...] = kn.astype(BF16)

    @pl.when(jk == 5)
    def _():
        win_ref[...] = acc
        winb_ref[...] = acc.astype(BF16)

    @pl.when(jk == 6)
    def _():
        gt_ref[...] = jax.nn.sigmoid(acc)


def _inproj(x2d, x_spec, sh, sc, mod_spec, n_rows, tm, g1, w_in_p, g_q, g_ks, g_kw, d_ssm):
    d = g1.shape[-1]
    nu = d_ssm // PROJ_TN
    nj = w_in_p.shape[1] // PROJ_TN
    assert nj == 2 * nu + 7
    ni = n_rows // tm
    clip = lambda v, lo, hi: jnp.minimum(jnp.maximum(v, lo), hi)
    vec = lambda n: pl.BlockSpec((1, n), lambda i, j: (0, 0))
    outs = [
        (jax.ShapeDtypeStruct((n_rows, d_ssm), F32), lambda i, j: (i, jnp.minimum(j, nu - 1))),
        (jax.ShapeDtypeStruct((n_rows, d_ssm), BF16), lambda i, j: (i, clip(j - nu, 0, nu - 1))),
        (jax.ShapeDtypeStruct((n_rows, 4 * KV_W), F32), lambda i, j: (i, clip(j - 2 * nu, 0, 3))),
        (jax.ShapeDtypeStruct((n_rows, 2 * KV_W), F32), lambda i, j: (i, clip(j - 2 * nu - 4, 0, 1))),
        (jax.ShapeDtypeStruct((n_rows, PROJ_TN), F32), lambda i, j: (i, 0)),
        (jax.ShapeDtypeStruct((n_rows, 4 * KV_W), BF16), lambda i, j: (i, clip(j - 2 * nu, 0, 3))),
        (jax.ShapeDtypeStruct((n_rows, 2 * KV_W), BF16), lambda i, j: (i, clip(j - 2 * nu - 4, 0, 1))),
    ]
    cmp_shape = jax.ShapeDtypeStruct((2 * N_KV, n_rows, HEAD_DIM), F32)
    cmp_spec = pl.BlockSpec((N_KV, tm, HEAD_DIM), lambda i, j: (clip(j - 2 * nu, 0, 1), i, 0))
    return pl.pallas_call(
        functools.partial(_inproj_body, nu=nu),
        grid=(ni, nj),
        in_specs=[x_spec, mod_spec, mod_spec, vec(d),
                  pl.BlockSpec((d, PROJ_TN), lambda i, j: (0, j)),
                  vec(HEAD_DIM), vec(HEAD_DIM), vec(HEAD_DIM)],
        out_specs=[pl.BlockSpec((tm, PROJ_TN), im) for _, im in outs] + [cmp_spec],
        out_shape=[s for s, _ in outs] + [cmp_shape],
        scratch_shapes=[pltpu.VMEM((tm, d), BF16)],
        compiler_params=_cparams(("arbitrary", "arbitrary"), 56),
        name="inproj",
    )(x2d, sh, sc, g1, w_in_p, g_q, g_ks, g_kw)


def _disc_body(are_ref, aim_ref, ldt_ref, br_ref, bi_ref, abr_ref, abi_ref, bbr_ref, bbi_ref):
    lam_re = jnp.minimum(are_ref[...], -1e-4)
    lam_im = aim_ref[...]
    dt = jnp.exp(ldt_ref[...])
    mag = jnp.exp(lam_re * dt)
    ab_re = mag * jnp.cos(lam_im * dt)
    ab_im = mag * jnp.sin(lam_im * dt)
    den = lam_re * lam_re + lam_im * lam_im
    n_re = ab_re - 1.0
    f_re = (n_re * lam_re + ab_im * lam_im) / den
    f_im = (ab_im * lam_re - n_re * lam_im) / den
    abr_ref[...] = ab_re
    abi_ref[...] = ab_im
    br = br_ref[...]
    bi = bi_ref[...]
    bbr_ref[...] = f_re[None] * br - f_im[None] * bi
    bbi_ref[...] = f_re[None] * bi + f_im[None] * br


def _discretize(a_re, a_im, log_dt, b_re, b_im):
    g, p = a_re.shape
    c = b_re.shape[0]
    gp = jax.ShapeDtypeStruct((g, p), F32)
    cgp = jax.ShapeDtypeStruct((c, g, p), F32)
    return pl.pallas_call(_disc_body, out_shape=[gp, gp, cgp, cgp], name="s5_disc")(
        a_re, a_im, log_dt.reshape(g, 1), b_re, b_im)


def _cscan_step(ar, ai, hr, hi, br, bi):
    return ar * hr - ai * hi + br, ar * hi + ai * hr + bi


def _s5_prompt_body(u_ref, bd_ref, cd_ref, are_ref, aim_ref, d_ref, h0r_ref, h0i_ref,
                    z_ref, sre_ref, sim_ref,
                    lhs_scr, bu_scr, y_scr, end_scr, init_scr, *, seq, nseg, ic):
    seg = seq // nseg
    nst = S5_CHUNK_ST
    half = min(nst, 512)
    nparts = nst // half
    rows_c = ic * nseg
    n_chunks = seg // ic

    def perm(i, _):
        lhs_scr[pl.ds(pl.multiple_of(i * nseg, nseg), nseg), :] = u_ref[pl.ds(i, nseg, stride=seg), :]
        return 0
    lax.fori_loop(0, seg, perm, 0)

    a_re = are_ref[...]
    a_im = aim_ref[...]

    def scan_chunk(ci, carry, store):
        r0 = pl.multiple_of(ci * rows_c, rows_c)
        lhs = lhs_scr[pl.ds(r0, rows_c), :].astype(BF16)
        bu_scr[...] = jnp.dot(lhs, bd_ref[...], preferred_element_type=F32)
        out = []
        for hf in range(nparts):
            lo = hf * half
            ar = jnp.broadcast_to(a_re[:, lo:lo + half], (nseg, half))
            ai = jnp.broadcast_to(a_im[:, lo:lo + half], (nseg, half))

            def step(i, c, lo=lo, ar=ar, ai=ai):
                rr = pl.multiple_of(i * nseg, nseg)
                br = bu_scr[pl.ds(rr, nseg), lo:lo + half]
                bi = bu_scr[pl.ds(rr, nseg), nst + lo:nst + lo + half]
                nr, ni = _cscan_step(ar, ai, c[0], c[1], br, bi)
                if store:
                    bu_scr[pl.ds(rr, nseg), lo:lo + half] = nr
                    bu_scr[pl.ds(rr, nseg), nst + lo:nst + lo + half] = ni
                return nr, ni
            out.append(lax.fori_loop(0, ic, step, carry[hf]))
        return tuple(out)

    row0 = lax.broadcasted_iota(I32, (nseg, half), 0) == 0
    h0r = h0r_ref[...]
    h0i = h0i_ref[...]
    init1 = tuple((jnp.where(row0, h0r[:, hf * half:(hf + 1) * half], 0.0),
                   jnp.where(row0, h0i[:, hf * half:(hf + 1) * half], 0.0)) for hf in range(nparts))
    ends = lax.fori_loop(0, n_chunks, lambda ci, c: scan_chunk(ci, c, False), init1)
    for hf in range(nparts):
        end_scr[:, hf * half:(hf + 1) * half] = ends[hf][0]
        end_scr[:, nst + hf * half:nst + (hf + 1) * half] = ends[hf][1]

    pr, pi_ = a_re, a_im
    for _ in range(int(math.log2(seg))):
        pr, pi_ = pr * pr - pi_ * pi_, 2.0 * pr * pi_
    t_re, t_im = h0r, h0i
    for j in range(nseg):
        init_scr[j:j + 1, 0:nst] = t_re
        init_scr[j:j + 1, nst:2 * nst] = t_im
        e_re = end_scr[j:j + 1, 0:nst]
        e_im = end_scr[j:j + 1, nst:2 * nst]
        if j == 0:
            t_re, t_im = e_re, e_im
        else:
            t_re, t_im = e_re + pr * t_re - pi_ * t_im, e_im + pr * t_im + pi_ * t_re
    sre_ref[...] = t_re
    sim_ref[...] = t_im

    init2 = tuple((init_scr[:, hf * half:(hf + 1) * half],
                   init_scr[:, nst + hf * half:nst + (hf + 1) * half]) for hf in range(nparts))

    def chunk2(ci, carry):
        carry = scan_chunk(ci, carry, True)
        y_scr[...] = jnp.dot(bu_scr[...].astype(BF16), cd_ref[...], preferred_element_type=F32)

        def unperm(i, _):
            rows = y_scr[pl.ds(pl.multiple_of(i * nseg, nseg), nseg), :]
            z_ref[pl.ds(ci * ic + i, nseg, stride=seg), :] = rows
            return 0
        lax.fori_loop(0, ic, unperm, 0)
        return carry
    lax.fori_loop(0, n_chunks, chunk2, init2)

    d = d_ref[...]

    def fin(k, _):
        r0 = pl.multiple_of(k * 512, 512)
        z_ref[pl.ds(r0, 512), :] = jax.nn.gelu(z_ref[pl.ds(r0, 512), :] + d * u_ref[pl.ds(r0, 512), :])
        return 0
    lax.fori_loop(0, seq // 512, fin, 0)


def _s5_prompt(u, bd, cd, ab_re, ab_im, d_skip, h0_re, h0_im, batch, seq):
    nck = bd.shape[0]
    nseg, ic = 8, 64
    assert seq % (nseg * ic) == 0 and (seq // nseg) & (seq // nseg - 1) == 0
    nst = S5_CHUNK_ST
    st = jax.ShapeDtypeStruct((batch, 1, nck * nst), F32)
    st_spec = pl.BlockSpec((None, 1, nst), lambda b, c: (b, 0, c))
    return pl.pallas_call(
        functools.partial(_s5_prompt_body, seq=seq, nseg=nseg, ic=ic),
        grid=(batch, nck),
        in_specs=[pl.BlockSpec((seq, S5_CHUNK_CH), lambda b, c: (b, c)),
                  pl.BlockSpec((None, S5_CHUNK_CH, 2 * nst), lambda b, c: (c, 0, 0)),
                  pl.BlockSpec((None, 2 * nst, S5_CHUNK_CH), lambda b, c: (c, 0, 0)),
                  pl.BlockSpec((None, 1, nst), lambda b, c: (c, 0, 0)),
                  pl.BlockSpec((None, 1, nst), lambda b, c: (c, 0, 0)),
                  pl.BlockSpec((1, S5_CHUNK_CH), lambda b, c: (0, c)),
                  st_spec, st_spec],
        out_specs=[pl.BlockSpec((seq, S5_CHUNK_CH), lambda b, c: (b, c)), st_spec, st_spec],
        out_shape=[jax.ShapeDtypeStruct((batch * seq, nck * S5_CHUNK_CH), F32), st, st],
        scratch_shapes=[pltpu.VMEM((seq, S5_CHUNK_CH), F32),
                        pltpu.VMEM((ic * nseg, 2 * nst), F32),
                        pltpu.VMEM((ic * nseg, S5_CHUNK_CH), F32),
                        pltpu.VMEM((nseg, 2 * nst), F32),
                        pltpu.VMEM((nseg, 2 * nst), F32)],
        compiler_params=_cparams(("arbitrary", "arbitrary"), 48),
        name="s5_prompt",
    )(u, bd, cd, ab_re, ab_im, d_skip, h0_re, h0_im)


def _s5_sample_body(u_ref, bd_ref, cd_ref, are_ref, aim_ref, d_ref, h0r_ref, h0i_ref,
                    z_ref, sre_ref, sim_ref, bu_scr, *, nb, nt):
    nst = S5_CHUNK_ST
    bu_scr[...] = jnp.dot(u_ref[...], bd_ref[...], preferred_element_type=F32, precision=HIGHEST)
    ar = are_ref[...]
    ai = aim_ref[...]
    hr = h0r_ref[...]
    hi = h0i_ref[...]
    for t in range(nt):
        rows = slice(t * nb, (t + 1) * nb)
        hr, hi = _cscan_step(ar, ai, hr, hi, bu_scr[rows, 0:nst], bu_scr[rows, nst:2 * nst])
        bu_scr[rows, 0:nst] = hr
        bu_scr[rows, nst:2 * nst] = hi
    sre_ref[...] = hr
    sim_ref[...] = hi
    y = jnp.dot(bu_scr[...].astype(BF16), cd_ref[...], preferred_element_type=F32)
    z_ref[...] = jax.nn.gelu(y + d_ref[...] * u_ref[...])


def _s5_sample(u, bd32, cd, ab_re, ab_im, d_skip, h0_re, h0_im, nb, nt):
    nck = bd32.shape[0]
    nst = S5_CHUNK_ST
    st = jax.ShapeDtypeStruct((nb, nck * nst), F32)
    st_spec = pl.BlockSpec((nb, nst), lambda c: (0, c))
    return pl.pallas_call(
        functools.partial(_s5_sample_body, nb=nb, nt=nt),
        grid=(nck,),
        in_specs=[pl.BlockSpec((nb * nt, S5_CHUNK_CH), lambda c: (0, c)),
                  pl.BlockSpec((None, S5_CHUNK_CH, 2 * nst), lambda c: (c, 0, 0)),
                  pl.BlockSpec((None, 2 * nst, S5_CHUNK_CH), lambda c: (c, 0, 0)),
                  pl.BlockSpec((None, 1, nst), lambda c: (c, 0, 0)),
                  pl.BlockSpec((None, 1, nst), lambda c: (c, 0, 0)),
                  pl.BlockSpec((1, S5_CHUNK_CH), lambda c: (0, c)),
                  st_spec, st_spec],
        out_specs=[pl.BlockSpec((nb * nt, S5_CHUNK_CH), lambda c: (0, c)), st_spec, st_spec],
        out_shape=[jax.ShapeDtypeStruct((nb * nt, nck * S5_CHUNK_CH), F32), st, st],
        scratch_shapes=[pltpu.VMEM((nb * nt, 2 * nst), F32)],
        compiler_params=_cparams(("arbitrary",), 40),
        name="s5_sample",
    )(u, bd32, cd, ab_re, ab_im, d_skip, h0_re, h0_im)


def _s5_weights(ab_re, ab_im, bb_re, bb_im, c_re, c_im):
    g, p = ab_re.shape
    gl = S5_CHUNK_CH // SSM_CH
    nck = g // gl
    eye = jnp.eye(gl, dtype=F32)

    def bdiag(bb):
        x = bb.reshape(SSM_CH, nck, gl, p).transpose(1, 2, 0, 3)
        return jnp.einsum('kgcp,gh->kgchp', x, eye).reshape(nck, gl * SSM_CH, gl * p)

    def cdiag(cc):
        x = cc.reshape(nck, gl, SSM_CH, p)
        return jnp.einsum('kgcp,gh->kgphc', x, eye).reshape(nck, gl * p, gl * SSM_CH)

    bd = jnp.concatenate([bdiag(bb_re), bdiag(bb_im)], axis=-1)
    cd = jnp.concatenate([cdiag(c_re), -cdiag(c_im)], axis=1)
    return bd, cd.astype(BF16), ab_re.reshape(nck, 1, gl * p), ab_im.reshape(nck, 1, gl * p)


def _glu_body(z_ref, w_ref, b_ref, g_ref, o_ref):
    z = z_ref[...]
    gate = jnp.dot(z.astype(BF16), w_ref[...], preferred_element_type=F32) + b_ref[...]
    o_ref[...] = _rms(z * jax.nn.sigmoid(gate), g_ref[...]).astype(BF16)


def _glu(z, w_glu_b, b_glu, g_out, tm):
    n, d = z.shape
    vec = pl.BlockSpec((1, d), lambda i: (0, 0))
    return pl.pallas_call(
        _glu_body,
        grid=(n // tm,),
        in_specs=[pl.BlockSpec((tm, d), lambda i: (i, 0)),
                  pl.BlockSpec((d, d), lambda i: (0, 0)), vec, vec],
        out_specs=pl.BlockSpec((tm, d), lambda i: (i, 0)),
        out_shape=jax.ShapeDtypeStruct((n, d), BF16),
        compiler_params=_cparams(("arbitrary",), 40),
        name="s5_glu",
    )(z, w_glu_b, b_glu, g_out)


def _compress_rows(pe_ref, w1_ref, nblk, head_slices):
    hidden = w1_ref.shape[-1]

    def pair(sp, acc):
        xs = []
        for ds_ in range(2):
            s = 2 * sp + ds_
            pe = pe_ref[pl.ds(s, 1), :]
            xs.append(jnp.concatenate(
                [hs[pl.ds(s, nblk, stride=BLOCK), :] + pe for hs in head_slices], axis=0))
        x = jnp.concatenate(xs, axis=-1).astype(BF16)
        w = w1_ref[pl.ds(pl.multiple_of(sp * 2 * HEAD_DIM, 2 * HEAD_DIM), 2 * HEAD_DIM), :]
        return acc + jnp.dot(x, w, preferred_element_type=F32)

    acc0 = jnp.zeros((len(head_slices) * nblk, hidden), F32)
    return lax.fori_loop(0, BLOCK // 2, pair, acc0)


def _compress_body(rows_ref, pe_ref, w1_ref, w2_ref, g_ref, o_ref, *, nblk):
    slot = pl.program_id(1)
    heads = [rows_ref.at[h] for h in range(N_KV)]
    hid = jax.nn.gelu(_compress_rows(pe_ref, w1_ref, nblk, heads))
    out = jnp.dot(hid.astype(BF16), w2_ref[...], preferred_element_type=F32)

    @pl.when(slot == 0)
    def _():
        o_ref[...] = _rms(out, g_ref[...]).reshape(N_KV, nblk, HEAD_DIM)

    @pl.when(slot != 0)
    def _():
        o_ref[...] = out.reshape(N_KV, nblk, HEAD_DIM)


def _compress_prompt(kv_heads, pe, w1, w2, g_kc, batch, seq):
    nblk = seq // BLOCK
    hidden = w1.shape[-1]
    return pl.pallas_call(
        functools.partial(_compress_body, nblk=nblk),
        grid=(batch, 2),
        in_specs=[pl.BlockSpec((N_KV, seq, HEAD_DIM), lambda b, s: (s, b, 0)),
                  pl.BlockSpec((None, BLOCK, HEAD_DIM), lambda b, s: (s, 0, 0)),
                  pl.BlockSpec((None, BLOCK * HEAD_DIM, hidden), lambda b, s: (s, 0, 0)),
                  pl.BlockSpec((None, hidden, HEAD_DIM), lambda b, s: (s, 0, 0)),
                  pl.BlockSpec((1, HEAD_DIM), lambda b, s: (0, 0))],
        out_specs=pl.BlockSpec((None, None, N_KV, nblk, HEAD_DIM), lambda b, s: (b, s, 0, 0, 0)),
        out_shape=jax.ShapeDtypeStruct((batch, 2, N_KV, nblk, HEAD_DIM), F32),
        compiler_params=_cparams(("arbitrary", "arbitrary"), 48),
        name="nsa_compress",
    )(kv_heads, pe, w1, w2, g_kc)


def _softmax_rows(s):
    e = jnp.exp(s - jnp.max(s, axis=-1, keepdims=True))
    return e / jnp.sum(e, axis=-1, keepdims=True)


def _select_blocks(imp, blk, qblk, nb_real, axis=0):
    forced = (blk == 0) | (blk == qblk) | (blk == qblk - 1)
    causal = blk <= qblk
    score = jnp.where(causal, imp + jnp.where(forced, FORCE_BONUS, 0.0), NEG)
    cnt = jnp.zeros(score.shape, F32)
    for m in range(nb_real):
        row = score[m:m + 1, :] if axis == 0 else score[:, m:m + 1]
        beats = (row > score) | ((row == score) & (blk > m))
        cnt = cnt + jnp.where(beats, 1.0, 0.0)
    return jnp.where((cnt < N_SEL) & causal, 1.0, 0.0)


def _nsa_prompt_body(q_ref, gt_ref, kc_ref, vc_ref, ks_ref, vs_ref, kw_ref, vw_ref, e_ref,
                     o_ref, m_scr, l_scr, acc_scr, *, seq, nblk, wkeys):
    i = pl.program_id(2)
    qpos0 = i * Q_TILE
    q = q_ref[...]
    qs = [q[:, r * HEAD_DIM:(r + 1) * HEAD_DIM] for r in range(GQ)]
    q_all = jnp.concatenate(qs, axis=0)

    kc = kc_ref[...]
    s_t = lax.dot_general(kc, q_all.astype(F32), (((1,), (1,)), ((), ())),
                          preferred_element_type=F32, precision=HIGHEST)
    blk4 = lax.broadcasted_iota(I32, s_t.shape, 0)
    qpos4 = qpos0 + lax.broadcasted_iota(I32, s_t.shape, 1) % Q_TILE
    vis = (blk4 * BLOCK + (BLOCK - 1)) <= qpos4
    s_m = jnp.where(vis, s_t, NEG)
    e = jnp.exp(s_m - jnp.max(s_m, axis=0, keepdims=True))
    p_t = e / jnp.sum(e, axis=0, keepdims=True)
    p_t = p_t * jnp.where(vis, 1.0, 0.0)
    o_c = jnp.dot(p_t.T.astype(BF16), vc_ref[...].astype(BF16), preferred_element_type=F32)

    imp = p_t[:, 0:Q_TILE]
    for r in range(1, GQ):
        imp = imp + p_t[:, r * Q_TILE:(r + 1) * Q_TILE]
    blk = lax.broadcasted_iota(I32, imp.shape, 0)
    qpos_l = qpos0 + lax.broadcasted_iota(I32, imp.shape, 1)
    sel = _select_blocks(imp, blk, qpos_l // BLOCK, nblk)
    sel_q = sel.T.astype(BF16)

    qpos = qpos0 + lax.broadcasted_iota(I32, (Q_TILE, 1), 0)
    m_scr[...] = jnp.full(m_scr.shape, NEG, F32)
    l_scr[...] = jnp.zeros(l_scr.shape, F32)
    acc_scr[...] = jnp.zeros(acc_scr.shape, F32)

    def key_tile(kt, _):
        k0 = pl.multiple_of(kt * KEY_TILE, KEY_TILE)
        picked = jnp.dot(sel_q, e_ref[kt], preferred_element_type=F32)
        kpos = k0 + lax.broadcasted_iota(I32, (Q_TILE, KEY_TILE), 1)
        valid = (picked > 0.5) & (kpos <= qpos)
        k_t = ks_ref[pl.ds(k0, KEY_TILE), :]
        v_t = vs_ref[pl.ds(k0, KEY_TILE), :]
        for r in range(GQ):
            s = lax.dot_general(qs[r], k_t, (((1,), (1,)), ((), ())), preferred_element_type=F32)
            s = jnp.where(valid, s, NEG)
            m_old = m_scr[r]
            m_new = jnp.maximum(m_old, jnp.max(s, axis=-1, keepdims=True))
            alpha = jnp.exp(m_old - m_new)
            p = jnp.exp(s - m_new)
            l_scr[r] = alpha * l_scr[r] + jnp.sum(p, axis=-1, keepdims=True)
            acc_scr[r] = alpha * acc_scr[r] + jnp.dot(p.astype(BF16), v_t, preferred_element_type=F32)
            m_scr[r] = m_new
        return 0
    lax.fori_loop(0, (qpos0 + Q_TILE + KEY_TILE - 1) // KEY_TILE, key_tile, 0)

    w0 = pl.multiple_of(jnp.maximum(qpos0 + Q_TILE - wkeys, 0), Q_TILE)
    k_w = kw_ref[pl.ds(w0, wkeys), :]
    v_w = vw_ref[pl.ds(w0, wkeys), :]
    wpos = w0 + lax.broadcasted_iota(I32, (Q_TILE, wkeys), 1)
    w_ok = (wpos <= qpos) & (wpos >= qpos - WINDOW)

    g = gt_ref[...]
    for r in range(GQ):
        s_w = lax.dot_general(qs[r], k_w, (((1,), (1,)), ((), ())), preferred_element_type=F32)
        p_w = _softmax_rows(jnp.where(w_ok, s_w, NEG))
        o_w = jnp.dot(p_w.astype(BF16), v_w, preferred_element_type=F32)
        o_s = acc_scr[r] / l_scr[r]
        o = (g[:, 3 * r:3 * r + 1] * o_c[r * Q_TILE:(r + 1) * Q_TILE]
             + g[:, 3 * r + 1:3 * r + 2] * o_s + g[:, 3 * r + 2:3 * r + 3] * o_w)
        o_ref[:, r * HEAD_DIM:(r + 1) * HEAD_DIM] = o


def _block_expand(n_keys):
    kt = n_keys // KEY_TILE
    key_blk = (jnp.arange(kt)[:, None, None] * KEY_TILE + jnp.arange(KEY_TILE)[None, None, :]) // BLOCK
    return (key_blk == jnp.arange(LANE)[None, :, None]).astype(BF16)


def _nsa_prompt(q, gates, cmp_kv, kvb, winb, batch, seq):
    nblk = seq // BLOCK
    assert nblk <= LANE and seq % KEY_TILE == 0
    nq = seq // Q_TILE
    wkeys = min(WINDOW + Q_TILE, seq)
    kc = jnp.pad(cmp_kv, ((0, 0), (0, 0), (0, 0), (0, LANE - nblk), (0, 0)))
    kvcol = lambda c0: pl.BlockSpec((seq, HEAD_DIM), lambda b, g, i, c0=c0: (b, c0 + g))
    return pl.pallas_call(
        functools.partial(_nsa_prompt_body, seq=seq, nblk=nblk, wkeys=wkeys),
        grid=(batch, N_KV, nq),
        in_specs=[pl.BlockSpec((Q_TILE, GQ * HEAD_DIM), lambda b, g, i: (b * nq + i, g)),
                  pl.BlockSpec((Q_TILE, LANE), lambda b, g, i: (b * nq + i, g)),
                  pl.BlockSpec((None, None, None, LANE, HEAD_DIM), lambda b, g, i: (b, 0, g, 0, 0)),
                  pl.BlockSpec((None, None, None, LANE, HEAD_DIM), lambda b, g, i: (b, 1, g, 0, 0)),
                  kvcol(2 * N_KV), kvcol(3 * N_KV), kvcol(0), kvcol(N_KV),
                  pl.BlockSpec((seq // KEY_TILE, LANE, KEY_TILE), lambda b, g, i: (0, 0, 0))],
        out_specs=pl.BlockSpec((Q_TILE, GQ * HEAD_DIM), lambda b, g, i: (b * nq + i, g)),
        out_shape=jax.ShapeDtypeStruct((batch * seq, N_KV * GQ * HEAD_DIM), F32),
        scratch_shapes=[pltpu.VMEM((GQ, Q_TILE, 1), F32), pltpu.VMEM((GQ, Q_TILE, 1), F32),
                        pltpu.VMEM((GQ, Q_TILE, HEAD_DIM), F32)],
        compiler_params=_cparams(("arbitrary", "arbitrary", "arbitrary"), 40),
        name="nsa_prompt",
    )(q, gates, kc, kc, kvb, kvb, winb, winb, _block_expand(seq))


def _nsa_sample_body(pt_ref, pages_hbm, q_ref, gt_ref, kvn_ref, wn_ref, cw_ref,
                     pe_ref, w1_ref, w2_ref, gkc_ref, e_ref, o_ref, wo_ref,
                     cbuf, kbuf, kw_scr, cmp_scr, sem_c, sem_k, *, past, nt, wbuf, page, n_pages):
    s = pl.program_id(0)
    ns = pl.num_programs(0)
    slot = s % 2
    nb_tot = cbuf.shape[0] // BLOCK
    kp = kbuf.shape[1]
    wk = kw_scr.shape[0]
    rows_c = nb_tot * 8

    def page_dma(seq, part, dst, sem, start):
        def body(j, _):
            pg = pt_ref[seq * n_pages + j]
            cp = pltpu.make_async_copy(pages_hbm.at[pg, :, pl.ds(part * 8, 8), :],
                                       dst.at[pl.ds(pl.multiple_of(j * page, page), page)], sem)
            if start:
                cp.start()
            else:
                cp.wait()
            return 0
        lax.fori_loop(0, n_pages, body, 0)

    @pl.when(s == 0)
    def _():
        cbuf[past:] = jnp.zeros((cbuf.shape[0] - past, 8, HEAD_DIM), F32)
        kbuf[:, past:] = jnp.zeros((2, kp - past, 8, HEAD_DIM), F32)
        kw_scr[wbuf:] = jnp.zeros((wk - wbuf, 8, HEAD_DIM), F32)
        cmp_scr[...] = jnp.zeros(cmp_scr.shape, F32)
        page_dma(0, 0, cbuf, sem_c, True)
        page_dma(0, 1, kbuf.at[0], sem_k.at[0], True)

    kvn = kvn_ref[...]
    page_dma(s, 0, cbuf, sem_c, False)
    cbuf[past:past + nt] = kvn[:, 0:8, :]

    def kchunk(it, acc):
        xs = []
        for u in range(8):
            sidx = it * 8 + u
            x = cbuf[pl.ds(sidx, nb_tot, stride=BLOCK), :, :] + pe_ref[sidx]
            xs.append(x.reshape(rows_c, HEAD_DIM))
        x = jnp.concatenate(xs, axis=-1).astype(BF16)
        w = w1_ref[pl.ds(pl.multiple_of(it * 8 * HEAD_DIM, 8 * HEAD_DIM), 8 * HEAD_DIM), :]
        return acc + jnp.dot(x, w, preferred_element_type=F32)

    hidden = w1_ref.shape[1] // 2
    acc = lax.fori_loop(0, BLOCK // 8, kchunk, jnp.zeros((rows_c, 2 * hidden), F32))

    @pl.when(s + 1 < ns)
    def _():
        page_dma(s + 1, 0, cbuf, sem_c, True)

    is_k = (lax.broadcasted_iota(I32, (rows_c, 1), 0) % 8) < N_KV
    hid = jax.nn.gelu(jnp.where(is_k, acc[:, 0:hidden], acc[:, hidden:]))
    o2 = jnp.dot(hid.astype(BF16), w2_ref[...], preferred_element_type=F32)
    cmp_scr[0:rows_c, :] = jnp.where(is_k, _rms(o2[:, 0:HEAD_DIM], gkc_ref[...]), o2[:, HEAD_DIM:])

    page_dma(s, 1, kbuf.at[slot], sem_k.at[slot], False)

    @pl.when(s + 1 < ns)
    def _():
        page_dma(s + 1, 1, kbuf.at[1 - slot], sem_k.at[1 - slot], True)

    kbuf[slot, past:past + nt] = kvn[:, 8:16, :]
    cw = cw_ref[...]
    wn = wn_ref[...]
    kw_scr[0:wbuf] = cw
    kw_scr[wbuf:wbuf + nt] = wn
    wo_ref[0:wbuf - nt] = cw[nt:]
    wo_ref[wbuf - nt:] = wn

    rows = GQ * nt
    qpos = past + lax.broadcasted_iota(I32, (rows, 1), 0) % nt
    qf = q_ref[...]
    gt = gt_ref[...]
    blk = lax.broadcasted_iota(I32, (rows, LANE), 1)
    vis = (blk * BLOCK + (BLOCK - 1)) <= qpos
    blk_t = lax.broadcasted_iota(I32, (nt, LANE), 1)
    qblk_t = (past + lax.broadcasted_iota(I32, (nt, 1), 0)) // BLOCK
    kpos = lax.broadcasted_iota(I32, (rows, kp), 1)
    wlane = lax.broadcasted_iota(I32, (rows, wk), 1)
    wpos = (past - wbuf) + wlane
    w_ok = (wpos <= qpos) & (wpos >= qpos - WINDOW) & (wlane < wbuf + nt)
    nt_dot = (((1,), (1,)), ((), ()))
    kflat = kbuf.at[slot].reshape(kp * 8, HEAD_DIM)
    wflat = kw_scr.reshape(wk * 8, HEAD_DIM)

    for g in range(N_KV):
        qg32 = jnp.concatenate(
            [qf[:, (g * GQ + r) * HEAD_DIM:(g * GQ + r + 1) * HEAD_DIM] for r in range(GQ)], axis=0)
        qg = qg32.astype(BF16)
        kc = cmp_scr[pl.ds(g, LANE, stride=8), :]
        vc = cmp_scr[pl.ds(N_KV + g, LANE, stride=8), :]
        s_c = lax.dot_general(qg32, kc, nt_dot, preferred_element_type=F32, precision=HIGHEST)
        p_c = _softmax_rows(jnp.where(vis, s_c, NEG)) * jnp.where(vis, 1.0, 0.0)
        o_c = jnp.dot(p_c.astype(BF16), vc.astype(BF16), preferred_element_type=F32)
        imp = p_c[0:nt]
        for r in range(1, GQ):
            imp = imp + p_c[r * nt:(r + 1) * nt]
        sel = _select_blocks(imp, blk_t, qblk_t, nb_tot, axis=1)
        picked = jnp.dot(sel.astype(BF16), e_ref[...], preferred_element_type=F32)
        picked = jnp.concatenate([picked] * GQ, axis=0)
        valid = (picked > 0.5) & (kpos <= qpos)
        k_s = kflat[pl.ds(g, kp, stride=8), :].astype(BF16)
        v_s = kflat[pl.ds(N_KV + g, kp, stride=8), :].astype(BF16)
        s_s = lax.dot_general(qg, k_s, nt_dot, preferred_element_type=F32)
        p_s = _softmax_rows(jnp.where(valid, s_s, NEG))
        o_s = jnp.dot(p_s.astype(BF16), v_s, preferred_element_type=F32)
        k_w = wflat[pl.ds(g, wk, stride=8), :].astype(BF16)
        v_w = wflat[pl.ds(N_KV + g, wk, stride=8), :].astype(BF16)
        s_w = lax.dot_general(qg, k_w, nt_dot, preferred_element_type=F32)
        p_w = _softmax_rows(jnp.where(w_ok, s_w, NEG))
        o_w = jnp.dot(p_w.astype(BF16), v_w, preferred_element_type=F32)
        gcol = lambda br: jnp.concatenate(
            [gt[:, g * LANE + 3 * r + br:g * LANE + 3 * r + br + 1] for r in range(GQ)], axis=0)
        o = gcol(0) * o_c + gcol(1) * o_s + gcol(2) * o_w
        for r in range(GQ):
            h = g * GQ + r
            o_ref[:, h * HEAD_DIM:(h + 1) * HEAD_DIM] = o[r * nt:(r + 1) * nt]


def _nsa_sample(q_bt, gates_bt, kvn, wn, pages, cwin, page_table, cw2):
    nb, nt, _ = q_bt.shape
    n_pages = page_table.shape[1]
    page = pages.shape[1]
    past = n_pages * page
    wbuf = cwin.shape[1]
    assert past % BLOCK == 0 and past >= wbuf and nt <= 8 and wbuf > nt
    nb_tot = -(-(past + nt) // BLOCK)
    kp = -(-(past + nt) // LANE) * LANE
    wk = -(-(wbuf + nt) // LANE) * LANE
    assert nb_tot <= LANE
    pe8, w1cat, w2cat, g_kc = cw2
    key_blk = jnp.arange(kp)[None, :] // BLOCK
    expand = (key_blk == jnp.arange(LANE)[:, None]).astype(BF16)
    seq3 = lambda n: pl.BlockSpec((None, nt, n), lambda s, pt: (s, 0, 0))
    seq4 = lambda r, n: pl.BlockSpec((None, r, n, HEAD_DIM), lambda s, pt: (s, 0, 0, 0))
    const = lambda a: pl.BlockSpec(a.shape, lambda s, pt: (0,) * a.ndim)
    grid_spec = pltpu.PrefetchScalarGridSpec(
        num_scalar_prefetch=1,
        grid=(nb,),
        in_specs=[pl.BlockSpec(memory_space=pl.ANY),
                  seq3(N_KV * GQ * HEAD_DIM), seq3(N_KV * LANE), seq4(nt, 16), seq4(nt, 8), seq4(wbuf, 8),
                  const(pe8), pl.BlockSpec(w1cat.shape, lambda s, pt: (0, 0), pipeline_mode=pl.Buffered(1)),
                  const(w2cat), const(g_kc), const(expand)],
        out_specs=[pl.BlockSpec((None, nt, N_KV * GQ * HEAD_DIM), lambda s, pt: (s, 0, 0)), seq4(wbuf, 8)],
        scratch_shapes=[pltpu.VMEM((nb_tot * BLOCK, 8, HEAD_DIM), F32),
                        pltpu.VMEM((2, kp, 8, HEAD_DIM), F32),
                        pltpu.VMEM((wk, 8, HEAD_DIM), F32),
                        pltpu.VMEM((LANE * 8 + 8, HEAD_DIM), F32),
                        pltpu.SemaphoreType.DMA(()), pltpu.SemaphoreType.DMA((2,))])
    return pl.pallas_call(
        functools.partial(_nsa_sample_body, past=past, nt=nt, wbuf=wbuf, page=page, n_pages=n_pages),
        grid_spec=grid_spec,
        out_shape=[jax.ShapeDtypeStruct((nb, nt, N_KV * GQ * HEAD_DIM), F32),
                   jax.ShapeDtypeStruct(cwin.shape, F32)],
        compiler_params=_cparams(("arbitrary",), 58),
        name="nsa_sample",
    )(page_table.reshape(-1), pages, q_bt, gates_bt, kvn, wn, cwin, pe8, w1cat, w2cat, g_kc, expand)


def _nsa_sample_body_old(pt_ref, pg0_ref, pg1_ref, pg2_ref, pg3_ref, q_ref, gt_ref, kvn_ref, wn_ref, cw_ref,
                     pe_ref, w1_ref, w2_ref, gkc_ref, e_ref, o_ref,
                     cmp_scr, ks_scr, vs_scr, kw_scr, vw_scr, *, past, nt, wbuf, page, ppg):
    s = pl.program_id(0)
    pg = pl.program_id(1)
    nb_tot = -(-(past + nt) // BLOCK)
    nblk = cmp_scr.shape[1] // BLOCK
    kp = ks_scr.shape[0]
    wk = kw_scr.shape[0]

    @pl.when((s == 0) & (pg == 0))
    def _():
        cmp_scr[:, past:, :] = jnp.zeros((2 * N_KV, cmp_scr.shape[1] - past, HEAD_DIM), F32)
        ks_scr[past:, :] = jnp.zeros((kp - past, KV_W), BF16)
        vs_scr[past:, :] = jnp.zeros((kp - past, KV_W), BF16)
        kw_scr[wbuf:, :] = jnp.zeros((wk - wbuf, KV_W), BF16)
        vw_scr[wbuf:, :] = jnp.zeros((wk - wbuf, KV_W), BF16)

    for k, pref in enumerate((pg0_ref, pg1_ref, pg2_ref, pg3_ref)):
        r0 = pl.multiple_of((pg * ppg + k) * page, page)
        for h in range(2 * N_KV):
            cmp_scr[h, pl.ds(r0, page), :] = pref[:, h * HEAD_DIM:(h + 1) * HEAD_DIM]
        ks_scr[pl.ds(r0, page), :] = pref[:, 2 * KV_W:3 * KV_W].astype(BF16)
        vs_scr[pl.ds(r0, page), :] = pref[:, 3 * KV_W:4 * KV_W].astype(BF16)

    @pl.when(pg == pl.num_programs(1) - 1)
    def _():
        kvn = kvn_ref[...]
        for h in range(2 * N_KV):
            cmp_scr[h, past:past + nt, :] = kvn[:, h * HEAD_DIM:(h + 1) * HEAD_DIM]
        ks_scr[past:past + nt, :] = kvn[:, 2 * KV_W:3 * KV_W].astype(BF16)
        vs_scr[past:past + nt, :] = kvn[:, 3 * KV_W:4 * KV_W].astype(BF16)
        cw = cw_ref[...]
        wn = wn_ref[...]
        kw_scr[0:wbuf, :] = cw[:, 0:KV_W].astype(BF16)
        vw_scr[0:wbuf, :] = cw[:, KV_W:2 * KV_W].astype(BF16)
        kw_scr[wbuf:wbuf + nt, :] = wn[:, 0:KV_W].astype(BF16)
        vw_scr[wbuf:wbuf + nt, :] = wn[:, KV_W:2 * KV_W].astype(BF16)

        cmp_out = []
        for slot in range(2):
            heads = [cmp_scr.at[slot * N_KV + h] for h in range(N_KV)]
            hid = jax.nn.gelu(_compress_rows(pe_ref.at[slot], w1_ref.at[slot], nblk, heads))
            out = jnp.dot(hid.astype(BF16), w2_ref[slot], preferred_element_type=F32)
            cmp_out.append(_rms(out, gkc_ref[...]) if slot == 0 else out)
        zpad = jnp.zeros((LANE - nblk, HEAD_DIM), F32)

        rows = GQ * nt
        t_of_row = lax.broadcasted_iota(I32, (rows, 1), 0) % nt
        qpos = past + t_of_row
        qf = q_ref[...]
        gt = gt_ref[...]
        blk = lax.broadcasted_iota(I32, (rows, LANE), 1)
        vis = (blk * BLOCK + (BLOCK - 1)) <= qpos
        blk_t = lax.broadcasted_iota(I32, (nt, LANE), 1)
        qblk_t = (past + lax.broadcasted_iota(I32, (nt, 1), 0)) // BLOCK
        kpos = lax.broadcasted_iota(I32, (rows, kp), 1)
        wlane = lax.broadcasted_iota(I32, (rows, wk), 1)
        wpos = (past - wbuf) + wlane
        w_ok = (wpos <= qpos) & (wpos >= qpos - WINDOW) & (wlane < wbuf + nt)

        for g in range(N_KV):
            qg32 = jnp.concatenate(
                [qf[:, (g * GQ + r) * HEAD_DIM:(g * GQ + r + 1) * HEAD_DIM] for r in range(GQ)], axis=0)
            qg = qg32.astype(BF16)
            kc = jnp.concatenate([cmp_out[0][g * nblk:(g + 1) * nblk], zpad], axis=0)
            vc = jnp.concatenate([cmp_out[1][g * nblk:(g + 1) * nblk], zpad], axis=0)
            s_c = lax.dot_general(qg32, kc, (((1,), (1,)), ((), ())),
                                  preferred_element_type=F32, precision=HIGHEST)
            p_c = _softmax_rows(jnp.where(vis, s_c, NEG)) * jnp.where(vis, 1.0, 0.0)
            o_c = jnp.dot(p_c.astype(BF16), vc.astype(BF16), preferred_element_type=F32)
            imp = p_c[0:nt]
            for r in range(1, GQ):
                imp = imp + p_c[r * nt:(r + 1) * nt]
            sel = _select_blocks(imp, blk_t, qblk_t, nb_tot, axis=1)
            picked = jnp.dot(sel.astype(BF16), e_ref[...], preferred_element_type=F32)
            picked = jnp.concatenate([picked] * GQ, axis=0)
            valid = (picked > 0.5) & (kpos <= qpos)
            cols = slice(g * HEAD_DIM, (g + 1) * HEAD_DIM)
            s_s = lax.dot_general(qg, ks_scr[:, cols], (((1,), (1,)), ((), ())), preferred_element_type=F32)
            p_s = _softmax_rows(jnp.where(valid, s_s, NEG))
            o_s = jnp.dot(p_s.astype(BF16), vs_scr[:, cols], preferred_element_type=F32)
            s_w = lax.dot_general(qg, kw_scr[:, cols], (((1,), (1,)), ((), ())), preferred_element_type=F32)
            p_w = _softmax_rows(jnp.where(w_ok, s_w, NEG))
            o_w = jnp.dot(p_w.astype(BF16), vw_scr[:, cols], preferred_element_type=F32)
            gcol = lambda br: jnp.concatenate(
                [gt[:, g * LANE + 3 * r + br:g * LANE + 3 * r + br + 1] for r in range(GQ)], axis=0)
            o = gcol(0) * o_c + gcol(1) * o_s + gcol(2) * o_w
            for r in range(GQ):
                h = g * GQ + r
                o_ref[:, h * HEAD_DIM:(h + 1) * HEAD_DIM] = o[r * nt:(r + 1) * nt]


def _nsa_sample_old(q_bt, gates_bt, kvnew_bt, winnew_bt, cache_pages, cache_win2, page_table, cw):
    nb, nt, _ = q_bt.shape
    n_pages = page_table.shape[1]
    page = cache_pages.shape[1]
    past = n_pages * page
    wbuf = cache_win2.shape[1]
    ppg = 4
    assert n_pages % ppg == 0 and past % BLOCK == 0 and past >= wbuf and nt <= 8
    nb_tot = -(-(past + nt) // BLOCK)
    nblk = -(-nb_tot // 8) * 8
    kp = -(-(past + nt) // LANE) * LANE
    wk = -(-(wbuf + nt) // LANE) * LANE
    assert nblk <= LANE
    pe, w1, w2, g_kc = cw
    key_blk = jnp.arange(kp)[None, :] // BLOCK
    expand = (key_blk == jnp.arange(LANE)[:, None]).astype(BF16)
    seq3 = lambda n: pl.BlockSpec((None, nt, n), lambda s, g, pt: (s, 0, 0))
    const = lambda shape: pl.BlockSpec(shape, lambda s, g, pt: (0,) * len(shape))
    page_spec = lambda k: pl.BlockSpec(
        (None, page, 4 * KV_W), lambda s, g, pt, k=k: (pt[s * n_pages + g * ppg + k], 0, 0))
    grid_spec = pltpu.PrefetchScalarGridSpec(
        num_scalar_prefetch=1,
        grid=(nb, n_pages // ppg),
        in_specs=[page_spec(0), page_spec(1), page_spec(2), page_spec(3),
                  seq3(N_KV * GQ * HEAD_DIM), seq3(N_KV * LANE), seq3(4 * KV_W), seq3(2 * KV_W),
                  pl.BlockSpec((None, wbuf, 2 * KV_W), lambda s, g, pt: (s, 0, 0)),
                  const(pe.shape), const(w1.shape), const(w2.shape), const(g_kc.shape), const(expand.shape)],
        out_specs=pl.BlockSpec((None, nt, N_KV * GQ * HEAD_DIM), lambda s, g, pt: (s, 0, 0)),
        scratch_shapes=[pltpu.VMEM((2 * N_KV, nblk * BLOCK, HEAD_DIM), F32),
                        pltpu.VMEM((kp, KV_W), BF16), pltpu.VMEM((kp, KV_W), BF16),
                        pltpu.VMEM((wk, KV_W), BF16), pltpu.VMEM((wk, KV_W), BF16)])
    return pl.pallas_call(
        functools.partial(_nsa_sample_body_old, past=past, nt=nt, wbuf=wbuf, page=page, ppg=ppg),
        grid_spec=grid_spec,
        out_shape=jax.ShapeDtypeStruct((nb, nt, N_KV * GQ * HEAD_DIM), F32),
        compiler_params=_cparams(("arbitrary", "arbitrary"), 56),
        name="nsa_sample",
    )(page_table.reshape(-1), cache_pages, cache_pages, cache_pages, cache_pages,
      q_bt, gates_bt, kvnew_bt, winnew_bt, cache_win2, pe, w1, w2, g_kc, expand)


def _merge_body(sn_ref, at_ref, ga_ref, x_ref, w_ref, g_ref, o_ref, cat_scr):
    @pl.when(pl.program_id(1) == 0)
    def _():
        d2 = sn_ref.shape[1]
        cat_scr[:, 0:d2] = sn_ref[...]
        cat_scr[:, d2:] = _rms(at_ref[...], g_ref[...]).astype(BF16)

    acc = jnp.dot(cat_scr[...], w_ref[...], preferred_element_type=F32)
    o_ref[...] = x_ref[...] + ga_ref[...] * acc


def _merge(sn, attn, ga, ga_spec, x2d, x_spec, w_out_b, g_attn, n_rows, tm):
    d2 = sn.shape[1]
    d = w_out_b.shape[1]
    tn = 512
    return pl.pallas_call(
        _merge_body,
        grid=(n_rows // tm, d // tn),
        in_specs=[pl.BlockSpec((tm, d2), lambda i, j: (i, 0)),
                  pl.BlockSpec((tm, d2), lambda i, j: (i, 0)),
                  ga_spec, x_spec,
                  pl.BlockSpec((2 * d2, tn), lambda i, j: (0, j)),
                  pl.BlockSpec((1, d2), lambda i, j: (0, 0))],
        out_specs=pl.BlockSpec((tm, tn), lambda i, j: (i, j)),
        out_shape=jax.ShapeDtypeStruct((n_rows, d), F32),
        scratch_shapes=[pltpu.VMEM((tm, 2 * d2), BF16)],
        compiler_params=_cparams(("arbitrary", "arbitrary"), 48),
        name="merge",
    )(sn, attn, ga, x2d, w_out_b, g_attn)


SLAB = 40
SLAB_USED = 32


def _route_core(x, sh, sc, g2_ref, wr_ref, br_ref, h3_ref, rt_ref):
    tm, d = x.shape
    h = _rms(x, g2_ref[...]) * (1.0 + sc) + sh
    for j in range(d // LANE):
        h3_ref[pl.ds(j, tm, stride=SLAB), :] = h[:, j * LANE:(j + 1) * LANE]
    zero = jnp.zeros((tm, LANE), F32)
    for j in range(SLAB_USED, SLAB):
        h3_ref[pl.ds(j, tm, stride=SLAB), :] = zero
    logits = jnp.dot(h, wr_ref[...], preferred_element_type=F32, precision=HIGHEST) + br_ref[...]
    lane = lax.broadcasted_iota(I32, logits.shape, 1)
    big = jnp.int32(1 << 20)
    first = lambda mask: jnp.min(jnp.where(mask, lane, big), axis=-1, keepdims=True)
    is_g = lane < N_GROUPS
    gmax = jnp.max(jnp.where(is_g, logits, NEG), axis=-1, keepdims=True)
    g_sel = first(is_g & (logits == gmax))
    p_grp = 1.0 / jnp.sum(jnp.where(is_g, jnp.exp(logits - gmax), 0.0), axis=-1, keepdims=True)
    lo = N_GROUPS + EXPERTS_PER_GROUP * g_sel
    in_grp = (lane >= lo) & (lane < lo + EXPERTS_PER_GROUP)
    emax = jnp.max(jnp.where(in_grp, logits, NEG), axis=-1, keepdims=True)
    pe = jnp.where(in_grp, jnp.exp(logits - emax), 0.0)
    prob = pe / jnp.sum(pe, axis=-1, keepdims=True)
    p1 = jnp.max(jnp.where(in_grp, prob, -1.0), axis=-1, keepdims=True)
    i1 = first(in_grp & (prob == p1))
    rest = in_grp & (lane != i1)
    p2 = jnp.max(jnp.where(rest, prob, -1.0), axis=-1, keepdims=True)
    i2 = first(rest & (prob == p2))
    w1 = p_grp * p1 / (p1 + p2)
    w2 = p_grp * p2 / (p1 + p2)
    e1 = (i1 - N_GROUPS).astype(F32)
    e2 = (i2 - N_GROUPS).astype(F32)
    rt_ref[...] = jnp.where(lane == 0, e1, jnp.where(lane == 1, e2, jnp.where(lane == 2, w1, jnp.where(lane == 3, w2, 0.0))))


def _route_body(xp_ref, xs_ref, shp_ref, scp_ref, shs_ref, scs_ref, g2_ref, wr_ref, br_ref,
                h3_ref, rt_ref, *, ntp):
    i = pl.program_id(0)

    @pl.when(i < ntp)
    def _():
        _route_core(xp_ref[...], shp_ref[...], scp_ref[...], g2_ref, wr_ref, br_ref, h3_ref, rt_ref)

    @pl.when(i >= ntp)
    def _():
        _route_core(xs_ref[...], shs_ref[...], scs_ref[...], g2_ref, wr_ref, br_ref, h3_ref, rt_ref)


def _route(x1_p, x1_s, sh_p, sc_p, sh_s, sc_s, g2, w_r, b_r, seq, tm):
    n_p, d = x1_p.shape
    n_s = x1_s.shape[0]
    assert n_p % tm == 0 and n_s % tm == 0 and seq % tm == 0
    ntp, nts = n_p // tm, n_s // tm
    n_all = n_p + n_s
    tpb = seq // tm
    pidx = lambda i: jnp.minimum(i, ntp - 1)
    sidx = lambda i: jnp.maximum(i - ntp, 0)
    pmod = pl.BlockSpec((None, 1, d), lambda i: (pidx(i) // tpb, 0, 0))
    smod = pl.BlockSpec((tm, d), lambda i: (sidx(i), 0))
    return pl.pallas_call(
        functools.partial(_route_body, ntp=ntp),
        grid=(ntp + nts,),
        in_specs=[pl.BlockSpec((tm, d), lambda i: (pidx(i), 0)),
                  pl.BlockSpec((tm, d), lambda i: (sidx(i), 0)),
                  pmod, pmod, smod, smod,
                  pl.BlockSpec((1, d), lambda i: (0, 0)),
                  pl.BlockSpec((d, LANE), lambda i: (0, 0)),
                  pl.BlockSpec((1, LANE), lambda i: (0, 0))],
        out_specs=[pl.BlockSpec((tm * SLAB, LANE), lambda i: (i, 0)),
                   pl.BlockSpec((tm, LANE), lambda i: (i, 0))],
        out_shape=[jax.ShapeDtypeStruct((n_all * SLAB, LANE), F32),
                   jax.ShapeDtypeStruct((n_all, LANE), F32)],
        compiler_params=_cparams(("arbitrary",), 48),
        name="route",
    )(x1_p, x1_s, sh_p, sc_p, sh_s, sc_s, g2, w_r, b_r)


def _slab_dma(src_ref, src_row, dst_ref, dst_row, sem):
    return pltpu.make_async_copy(
        src_ref.at[pl.ds(pl.multiple_of(src_row * SLAB, 8), SLAB_USED), :],
        dst_ref.at[pl.ds(pl.multiple_of(dst_row * SLAB, 8), SLAB_USED), :], sem)


def _moe_body(te_ref, na_ref, tok_ref, h3_hbm, ws_ref, w1_ref, w3_ref, w2_ref, y_ref,
              xbuf, sem, a1_scr, a3_scr, hid_scr):
    t = pl.program_id(0)
    s = pl.program_id(1)
    tm = MOE_TM
    n_act = na_ref[0]
    slot = t % 2
    kc = w1_ref.shape[0]
    nk = kc // LANE

    def gather(tile, buf, start):
        def body(r, _):
            cp = _slab_dma(h3_hbm, tok_ref[tile * tm + r], xbuf.at[buf], r, sem.at[buf])
            if start:
                cp.start()
            else:
                cp.wait()
            return 0
        lax.fori_loop(0, tm, body, 0)

    @pl.when((s == 0) & (t >= n_act))
    def _():
        y_ref[...] = jnp.zeros(y_ref.shape, F32)

    @pl.when((s == 0) & (t == 0) & (n_act > 0))
    def _():
        gather(0, 0, True)

    @pl.when((s == 0) & (t < n_act))
    def _():
        gather(t, slot, False)

    @pl.when((s == 0) & (t + 1 < n_act))
    def _():
        gather(t + 1, 1 - slot, True)

    @pl.when((t < n_act) & (s < 4))
    def _():
        xb = xbuf.at[slot]
        acc1 = jnp.zeros((tm, w1_ref.shape[1]), F32)
        acc3 = jnp.zeros((tm, w3_ref.shape[1]), F32)
        for jj in range(0, nk, 2):
            x = jnp.concatenate(
                [xb[pl.ds(s * nk + jj + u, tm, stride=SLAB), :] for u in range(2)], axis=-1).astype(BF16)
            rows = slice(jj * LANE, (jj + 2) * LANE)
            acc1 = acc1 + jnp.dot(x, w1_ref[rows, :].astype(BF16), preferred_element_type=F32)
            acc3 = acc3 + jnp.dot(x, w3_ref[rows, :].astype(BF16), preferred_element_type=F32)

        @pl.when(s == 0)
        def _():
            a1_scr[...] = acc1
            a3_scr[...] = acc3

        @pl.when(s > 0)
        def _():
            a1_scr[...] += acc1
            a3_scr[...] += acc3

        @pl.when(s == 3)
        def _():
            a1 = a1_scr[...]
            hid = a1 * jax.nn.sigmoid(a1) * a3_scr[...]
            hid_scr[...] = (hid * ws_ref[...]).astype(BF16)

    @pl.when((t < n_act) & (s >= 4))
    def _():
        res = jnp.dot(hid_scr[...], w2_ref[...].astype(BF16), preferred_element_type=F32)
        nn = res.shape[1] // LANE
        for jj in range(nn):
            y_ref[pl.ds((s - 4) * nn + jj, tm, stride=SLAB), :] = res[:, jj * LANE:(jj + 1) * LANE]

        @pl.when(s == 4)
        def _():
            zero = jnp.zeros((tm, LANE), F32)
            for j in range(SLAB_USED, SLAB):
                y_ref[pl.ds(j, tm, stride=SLAB), :] = zero


def _moe(h3, tile_e, n_act, slot_tok, slot_w, w_e1, w_e3, w_e2, n_tiles):
    tm = MOE_TM
    _, d, f = w_e1.shape
    assert d // LANE == SLAB_USED and d % 4 == 0
    kc = d // 4
    act = lambda t, na: t < na[0]
    last = lambda t, na: jnp.minimum(t, jnp.maximum(na[0] - 1, 0))
    grid_spec = pltpu.PrefetchScalarGridSpec(
        num_scalar_prefetch=3,
        grid=(n_tiles, 8),
        in_specs=[pl.BlockSpec(memory_space=pl.ANY),
                  pl.BlockSpec((tm, 1), lambda t, s, te, na, tok: (last(t, na), 0)),
                  pl.BlockSpec((None, kc, f), lambda t, s, te, na, tok:
                               (te[t], jnp.where(act(t, na), jnp.minimum(s, 3), 3), 0)),
                  pl.BlockSpec((None, kc, f), lambda t, s, te, na, tok:
                               (te[t], jnp.where(act(t, na), jnp.minimum(s, 3), 3), 0)),
                  pl.BlockSpec((None, f, kc), lambda t, s, te, na, tok:
                               (te[t], 0, jnp.where(act(t, na), jnp.maximum(s - 4, 0), 3)))],
        out_specs=pl.BlockSpec((tm * SLAB, LANE), lambda t, s, te, na, tok: (t, 0)),
        scratch_shapes=[pltpu.VMEM((2, tm * SLAB, LANE), F32),
                        pltpu.SemaphoreType.DMA((2,)),
                        pltpu.VMEM((tm, f), F32), pltpu.VMEM((tm, f), F32), pltpu.VMEM((tm, f), BF16)])
    return pl.pallas_call(
        _moe_body,
        grid_spec=grid_spec,
        out_shape=jax.ShapeDtypeStruct((n_tiles * tm * SLAB, LANE), F32),
        compiler_params=_cparams(("arbitrary", "arbitrary"), 56),
        name="moe_ffn",
    )(tile_e, n_act, slot_tok, h3, slot_w, w_e1, w_e3, w_e2)


def _dispatch(rt, n_tiles):
    tm = MOE_TM
    n = rt.shape[0]
    e_flat = rt[:, 0:2].astype(I32).T.reshape(-1)
    w_flat = rt[:, 2:4].T.reshape(-1)
    onehot = (e_flat[:, None] == jnp.arange(N_EXPERTS, dtype=I32)[None, :]).astype(I32)
    csum = jnp.cumsum(onehot, axis=0)
    pos_in = jnp.sum(onehot * (csum - onehot), axis=1)
    counts = csum[-1]
    padded = (counts + tm - 1) // tm * tm
    start = jnp.cumsum(padded) - padded
    slot = start[e_flat] + pos_in
    n_slots = n_tiles * tm
    slot_tok = jnp.zeros((n_slots,), I32).at[slot].set(jnp.arange(2 * n, dtype=I32) % n)
    slot_w = jnp.zeros((n_slots,), F32).at[slot].set(w_flat)
    n_act = (jnp.sum(padded) // tm).astype(I32)
    tiles = jnp.arange(n_tiles, dtype=I32)
    tile_e = jnp.sum((start // tm)[None, :] <= tiles[:, None], axis=1).astype(I32) - 1
    tile_e = tile_e[jnp.minimum(tiles, jnp.maximum(n_act - 1, 0))]
    return tile_e, n_act.reshape(1), slot_tok, slot_w.reshape(n_slots, 1), slot.reshape(2, n)


def _combine_body(p0_ref, p1_ref, y_hbm, xp_ref, xs_ref, gp_ref, gs_ref, op_ref, os_ref,
                  ybuf, sem, *, ntp, tm):
    i = pl.program_id(0)
    n_i = pl.num_programs(0)
    slot = i % 2

    def gather(tile, buf, start):
        def body(r, _):
            for k, pref in enumerate((p0_ref, p1_ref)):
                cp = _slab_dma(y_hbm, pref[tile * tm + r], ybuf.at[buf, k], r, sem.at[buf])
                if start:
                    cp.start()
                else:
                    cp.wait()
            return 0
        lax.fori_loop(0, tm, body, 0)

    @pl.when(i == 0)
    def _():
        gather(0, 0, True)

    gather(i, slot, False)

    @pl.when(i + 1 < n_i)
    def _():
        gather(i + 1, 1 - slot, True)

    def emit(x_ref, g_ref, o_ref):
        for j in range(SLAB_USED):
            cols = slice(j * LANE, (j + 1) * LANE)
            y = (ybuf[slot, 0, pl.ds(j, tm, stride=SLAB), :] + ybuf[slot, 1, pl.ds(j, tm, stride=SLAB), :])
            o_ref[:, cols] = x_ref[:, cols] + g_ref[:, cols] * y

    @pl.when(i < ntp)
    def _():
        emit(xp_ref, gp_ref, op_ref)

    @pl.when(i >= ntp)
    def _():
        emit(xs_ref, gs_ref, os_ref)


def _combine(y_slabs, pos, x1_p, x1_s, ga_p, ga_s, seq, tm):
    n_p, d = x1_p.shape
    n_s = x1_s.shape[0]
    ntp, nts = n_p // tm, n_s // tm
    tpb = seq // tm
    pidx = lambda i: jnp.minimum(i, ntp - 1)
    sidx = lambda i: jnp.maximum(i - ntp, 0)
    grid_spec = pltpu.PrefetchScalarGridSpec(
        num_scalar_prefetch=2,
        grid=(ntp + nts,),
        in_specs=[pl.BlockSpec(memory_space=pl.ANY),
                  pl.BlockSpec((tm, d), lambda i, p0, p1: (pidx(i), 0)),
                  pl.BlockSpec((tm, d), lambda i, p0, p1: (sidx(i), 0)),
                  pl.BlockSpec((None, 1, d), lambda i, p0, p1: (pidx(i) // tpb, 0, 0)),
                  pl.BlockSpec((tm, d), lambda i, p0, p1: (sidx(i), 0))],
        out_specs=[pl.BlockSpec((tm, d), lambda i, p0, p1: (pidx(i), 0)),
                   pl.BlockSpec((tm, d), lambda i, p0, p1: (sidx(i), 0))],
        scratch_shapes=[pltpu.VMEM((2, 2, tm * SLAB, LANE), F32), pltpu.SemaphoreType.DMA((2,))])
    return pl.pallas_call(
        functools.partial(_combine_body, ntp=ntp, tm=tm),
        grid_spec=grid_spec,
        out_shape=[jax.ShapeDtypeStruct((n_p, d), F32), jax.ShapeDtypeStruct((n_s, d), F32)],
        compiler_params=_cparams(("arbitrary",), 56),
        name="combine",
    )(pos[0], pos[1], y_slabs, x1_p, x1_s, ga_p, ga_s)


def _prep_w_in(w_in):
    d = w_in.shape[0]
    n_main = w_in.shape[1] - 3 * N_KV * GQ
    gates = w_in[:, n_main:].reshape(d, N_KV, GQ * 3)
    gates = jnp.pad(gates, ((0, 0), (0, 0), (0, LANE - GQ * 3))).reshape(d, N_KV * LANE)
    return jnp.concatenate([w_in[:, :n_main], gates], axis=1).astype(BF16)


def _inproj_prompt(x_p, sh, sc, p, w_in_p):
    b, seq, d = x_p.shape
    tm = 512
    tpb = seq // tm
    mod = pl.BlockSpec((None, 1, d), lambda i, j: (i // tpb, 0, 0))
    return _inproj(x_p.reshape(b * seq, d), pl.BlockSpec((tm, d), lambda i, j: (i, 0)),
                   sh.reshape(b, 1, d), sc.reshape(b, 1, d), mod, b * seq, tm,
                   p['g_norm1'].reshape(1, d), w_in_p, p['g_q'].reshape(1, -1), p['g_ks'].reshape(1, -1),
                   p['g_kw'].reshape(1, -1), d // 2)


def _inproj_sample(x_s, sh, sc, p, w_in_p):
    nb, nt, d = x_s.shape
    mod = pl.BlockSpec((nb, d), lambda i, j: (0, 0))
    return _inproj(x_s.reshape(nb, nt * d), pl.BlockSpec((nb, d), lambda i, j: (0, i)),
                   sh, sc, mod, nb * nt, nb,
                   p['g_norm1'].reshape(1, d), w_in_p, p['g_q'].reshape(1, -1), p['g_ks'].reshape(1, -1),
                   p['g_kw'].reshape(1, -1), d // 2)


def _compress_prep(p):
    hidden = p['w_ck1'].shape[-1]
    pe = jnp.stack([p['pe_k'], p['pe_v']])
    w1 = jnp.stack([p['w_ck1'], p['w_cv1']]).reshape(2, BLOCK * HEAD_DIM, hidden).astype(BF16)
    w2 = jnp.stack([p['w_ck2'], p['w_cv2']]).astype(BF16)
    return pe, w1, w2, p['g_kc'].reshape(1, HEAD_DIM)


def _nsa_run_prompt(op, cw, batch, seq):
    _, q, _, _, gates, kvb, winb, cmp_hm = op
    cmp_kv = _compress_prompt(cmp_hm, *cw, batch, seq)
    return _nsa_prompt(q, gates, cmp_kv, kvb, winb, batch, seq)


def _nsa_run_sample(os_, cache_kv, cache_win, page_table, cw, nb, nt):
    _, q, kv, win, gates, _, _, _ = os_
    to_bt = lambda a: a.reshape(nt, nb, a.shape[-1]).transpose(1, 0, 2)
    pages = cache_kv.reshape(cache_kv.shape[0], cache_kv.shape[1], 4 * N_KV, HEAD_DIM)
    cwin = cache_win.reshape(nb, cache_win.shape[1], 2 * N_KV, HEAD_DIM)
    pe, w1, w2, g_kc = cw
    rep = lambda a: jnp.repeat(a[:, None, :], N_KV, axis=1)
    cw2 = (jnp.concatenate([rep(pe[0]), rep(pe[1])], axis=1),
           jnp.concatenate([w1[0], w1[1]], axis=1), jnp.concatenate([w2[0], w2[1]], axis=1), g_kc)
    attn, win_out = _nsa_sample(to_bt(q.astype(F32)), to_bt(gates),
                                to_bt(kv).reshape(nb, nt, 4 * N_KV, HEAD_DIM),
                                to_bt(win).reshape(nb, nt, 2 * N_KV, HEAD_DIM), pages, cwin, page_table, cw2)
    return attn.transpose(1, 0, 2).reshape(nt * nb, -1), win_out


def _s5_prep(p):
    ab_re, ab_im, bb_re, bb_im = _discretize(
        p['ssm_a_re'], p['ssm_a_im'], p['ssm_log_dt'],
        p['ssm_b_re'].transpose(2, 0, 1), p['ssm_b_im'].transpose(2, 0, 1))
    return _s5_weights(ab_re, ab_im, bb_re, bb_im, p['ssm_c_re'], p['ssm_c_im'])


def _s5_run_prompt(u, s5w, p, batch, seq):
    bd, cd, a_re, a_im = s5w
    d_ssm = u.shape[1]
    zeros = jnp.zeros((batch, 1, bd.shape[0] * S5_CHUNK_ST), F32)
    z, s_re, s_im = _s5_prompt(u, bd.astype(BF16), cd, a_re, a_im, p['ssm_d'].reshape(1, d_ssm),
                               zeros, zeros, batch, seq)
    sn = _glu(z, p['w_glu'].astype(BF16), p['b_glu'].reshape(1, d_ssm), p['g_out_ssm'].reshape(1, d_ssm), 512)
    return sn, s_re, s_im


def _s5_run_sample(u, s5w, p, st_re, st_im, nb, nt):
    bd, cd, a_re, a_im = s5w
    d_ssm = u.shape[1]
    z, s_re, s_im = _s5_sample(u, bd, cd, a_re, a_im, p['ssm_d'].reshape(1, d_ssm),
                               st_re.reshape(nb, -1), st_im.reshape(nb, -1), nb, nt)
    sn = _glu(z, p['w_glu'].astype(BF16), p['b_glu'].reshape(1, d_ssm), p['g_out_ssm'].reshape(1, d_ssm),
              min(512, nb * nt))
    return sn, s_re, s_im


def kernel(x_prompt, x_sample, c_prompt, c_sample, cache_kv, cache_win, state_ssm_re, state_ssm_im, page_table, w_ada, b_ada, g_norm1, g_norm2, w_in, g_q, g_kc, g_ks, g_kw, pe_k, pe_v, w_ck1, w_ck2, w_cv1, w_cv2, ssm_a_re, ssm_a_im, ssm_log_dt, ssm_b_re, ssm_b_im, ssm_c_re, ssm_c_im, ssm_d, w_glu, b_glu, g_out_ssm, g_out_attn, w_out, w_rg, b_rg, w_rexp, b_rexp, w_e1, w_e3, w_e2):
    names = ('w_ada', 'b_ada', 'g_norm1', 'g_norm2', 'w_in', 'g_q', 'g_kc', 'g_ks', 'g_kw', 'pe_k', 'pe_v',
             'w_ck1', 'w_ck2', 'w_cv1', 'w_cv2', 'ssm_a_re', 'ssm_a_im', 'ssm_log_dt', 'ssm_b_re', 'ssm_b_im',
             'ssm_c_re', 'ssm_c_im', 'ssm_d', 'w_glu', 'b_glu', 'g_out_ssm', 'g_out_attn', 'w_out',
             'w_rg', 'b_rg', 'w_rexp', 'b_rexp', 'w_e1', 'w_e3', 'w_e2')
    vals = (w_ada, b_ada, g_norm1, g_norm2, w_in, g_q, g_kc, g_ks, g_kw, pe_k, pe_v,
            w_ck1, w_ck2, w_cv1, w_cv2, ssm_a_re, ssm_a_im, ssm_log_dt, ssm_b_re, ssm_b_im,
            ssm_c_re, ssm_c_im, ssm_d, w_glu, b_glu, g_out_ssm, g_out_attn, w_out,
            w_rg, b_rg, w_rexp, b_rexp, w_e1, w_e3, w_e2)
    p = {k: v[0] for k, v in zip(names, vals)}
    b, seq, d = x_prompt.shape
    nb, nt, _ = x_sample.shape
    nc = b + nb
    c_all = jnp.concatenate([c_prompt, c_sample, jnp.zeros((-nc % 8, d), F32)], axis=0)
    m = _ada(c_all, p['w_ada'], p['b_ada'])
    mp, ms = m[:b], m[b:nc]
    w_in_p = _prep_w_in(p['w_in'])
    op = _inproj_prompt(x_prompt, mp[:, :d], mp[:, d:2 * d], p, w_in_p)
    os_ = _inproj_sample(x_sample, ms[:, :d], ms[:, d:2 * d], p, w_in_p)
    s5w = _s5_prep(p)
    sn_p, re_p, im_p = _s5_run_prompt(op[0], s5w, p, b, seq)
    sn_s, re_s, im_s = _s5_run_sample(os_[0], s5w, p, state_ssm_re[0], state_ssm_im[0], nb, nt)
    cw = _compress_prep(p)
    at_p = _nsa_run_prompt(op, cw, b, seq)
    at_s, win_s = _nsa_run_sample(os_, cache_kv[0], cache_win[0], page_table, cw, nb, nt)
    y_p, y_s = _tail(x_prompt, x_sample, mp, ms, sn_p, at_p, sn_s, at_s, p)

    to_bt = lambda a: a.reshape(nt, nb, a.shape[-1]).transpose(1, 0, 2)
    win_p = op[3].reshape(b, seq, 2, N_KV, HEAD_DIM)[:, seq - min(WINDOW, seq):]
    win_s = win_s.reshape(cache_win.shape[1:])
    g_p = p['ssm_a_re'].shape
    return (y_p.reshape(b, seq, d),
            to_bt(y_s),
            op[2].reshape(1, b, seq, 4, N_KV, HEAD_DIM),
            win_p[None],
            re_p.reshape(1, b, *g_p), im_p.reshape(1, b, *g_p),
            to_bt(os_[2]).reshape(1, nb, nt, 4, N_KV, HEAD_DIM),
            win_s[None],
            re_s.reshape(1, nb, *g_p), im_s.reshape(1, nb, *g_p))


def _tail(x_prompt, x_sample, mp, ms, sn_p, at_p, sn_s, at_s, p):
    b, seq, d = x_prompt.shape
    nb, nt, _ = x_sample.shape
    chunk = lambda a, k: a[:, k * d:(k + 1) * d]
    w_out_b = p['w_out'].astype(BF16)
    g_attn = p['g_out_attn'].reshape(1, -1)
    tn = 512
    tm_p = 512
    tpb = seq // tm_p
    x1_p = _merge(sn_p, at_p, chunk(mp, 2).reshape(b, 1, d),
                  pl.BlockSpec((None, 1, tn), lambda i, j: (i // tpb, 0, j)),
                  x_prompt.reshape(b * seq, d), pl.BlockSpec((tm_p, tn), lambda i, j: (i, j)),
                  w_out_b, g_attn, b * seq, tm_p)
    ncol = d // tn
    x1_s = _merge(sn_s, at_s, chunk(ms, 2), pl.BlockSpec((nb, tn), lambda i, j: (0, j)),
                  x_sample.reshape(nb, nt * d), pl.BlockSpec((nb, tn), lambda i, j: (0, i * ncol + j)),
                  w_out_b, g_attn, nt * nb, nb)

    tm = 128
    rep = lambda a: jnp.tile(a, (nt, 1))
    w_r = jnp.concatenate([p['w_rg'], p['w_rexp']], axis=1)
    b_r = jnp.concatenate([p['b_rg'], p['b_rexp']], axis=0)
    w_r = jnp.pad(w_r, ((0, 0), (0, LANE - w_r.shape[1])))
    b_r = jnp.pad(b_r, (0, LANE - b_r.shape[0])).reshape(1, LANE)
    h3, rt = _route(x1_p, x1_s, chunk(mp, 3).reshape(b, 1, d), chunk(mp, 4).reshape(b, 1, d),
                    rep(chunk(ms, 3)), rep(chunk(ms, 4)), p['g_norm2'].reshape(1, d), w_r, b_r, seq, tm)
    n_all = b * seq + nb * nt
    n_tiles = -(-2 * n_all // MOE_TM) + N_EXPERTS
    tile_e, n_act, slot_tok, slot_w, pos = _dispatch(rt, n_tiles)
    y_slabs = _moe(h3, tile_e, n_act, slot_tok, slot_w, p['w_e1'], p['w_e3'], p['w_e2'], n_tiles)
    return _combine(y_slabs, pos, x1_p, x1_s, chunk(mp, 5).reshape(b, 1, d), rep(chunk(ms, 5)), seq, tm)
```
